```python
import math
import jax
import jax.numpy as jnp
from jax import lax
import numpy as np

D_MODEL = 1024
BATCH = 16
SEQ = 2048
DEPTH = 4

GRID_W = 64
CTX_LEN = 256

D_MIX = D_MODEL
GROUP_W = D_MIX // 4
HY_W = GROUP_W
SC_W = GROUP_W
GLA_HEADS = 4
GLA_DV = GROUP_W // GLA_HEADS
GLA_DK = GLA_DV // 2
GLA_QK = GLA_HEADS * GLA_DK
GLA_V = GLA_HEADS * GLA_DV
GLA_RANK = 16
GLA_TAU = 16.0
GDN_HEADS = 4
GDN_DK = GROUP_W // GDN_HEADS
GDN_DV = GDN_DK
GDN_KW = GDN_HEADS * GDN_DK
GDN_VW = GDN_HEADS * GDN_DV
HY_EMB = 33
HY_HID = 64
HY_DECAY_TARGET = 1e-2
HY_FAST_PCT = 0.3
HY_SLOW_PCT = 1.5
CHUNK = 64
N_EXPERTS = 32
TOP_K = 4
D_EXPERT = D_MODEL
SWIGLU_LIMIT = 7.0
SWIGLU_ALPHA = 1.702
LN_EPS = 1e-5
RMS_EPS = 1e-6
L2_EPS = 1e-6
DN_ALPHA = (2 * DEPTH) ** 0.25
DN_BETA = (8 * DEPTH) ** -0.25

HY_COLS = 3 * HY_W
SC_COLS = 3 * SC_W
GLA_COLS = 2 * GLA_QK + 2 * GLA_V + 2 * GLA_RANK
GDN_COLS = 2 * GDN_KW + 2 * GDN_VW + 4 * GDN_HEADS
D_IN = HY_COLS + SC_COLS + GLA_COLS + GDN_COLS

kernel_name = 'hybrid_hyena_conv_gla_gdn_moe_diffusion_trunk'

F32 = jnp.float32


def _split(x, sizes):
    return jnp.split(x, [int(s) for s in np.cumsum(sizes)[:-1]], axis=-1)


def layer_norm(x, g, b):
    xf = x.astype(F32)
    xc = xf - jnp.mean(xf, axis=-1, keepdims=True)
    var = jnp.mean(xc * xc, axis=-1, keepdims=True)
    return (xc * lax.rsqrt(var + LN_EPS) * g.astype(F32) + b.astype(F32)).astype(x.dtype)


def rms_norm(x, w):
    xf = x.astype(F32)
    return xf * lax.rsqrt(jnp.mean(xf * xf, axis=-1, keepdims=True) + RMS_EPS) * w.astype(F32)


def l2_normalize(x):
    return x * lax.rsqrt(jnp.sum(x * x, axis=-1, keepdims=True) + L2_EPS)


def short_conv3(x, w, n_rows):
    bsz, L, ch = x.shape
    xr = x.reshape(bsz, n_rows, L // n_rows, ch)
    xp = jnp.pad(xr, ((0, 0), (0, 0), (1, 1), (0, 0)))
    y = xp[:, :, :-2] * w[0] + xp[:, :, 1:-1] * w[1] + xp[:, :, 2:] * w[2]
    return y.reshape(bsz, L, ch)


def hyena_filters(L, w1, b1, w2, b2, w3, b3, w4, freq):
    t = jnp.linspace(0.0, 1.0, L, dtype=F32)[:, None]
    bands = (HY_EMB - 1) // 2
    ang = 2.0 * math.pi * jnp.arange(L, dtype=F32)[:, None] / L
    f = jnp.linspace(1e-4, bands - 1, bands, dtype=F32)[None, :]
    z = jnp.concatenate([t, jnp.cos(f * ang), -jnp.sin(f * ang)], axis=-1).astype(w1.dtype)
    h = jnp.sin(freq[0] * (z @ w1 + b1))
    h = jnp.sin(freq[1] * (h @ w2 + b2))
    h = jnp.sin(freq[2] * (h @ w3 + b3))
    k = (h @ w4).astype(F32)
    max_decay = math.log(HY_DECAY_TARGET) / HY_FAST_PCT
    min_decay = math.log(HY_DECAY_TARGET) / HY_SLOW_PCT
    deltas = jnp.abs(jnp.linspace(min_decay, max_decay, HY_W, dtype=F32))
    window = jnp.exp(-t * deltas[None, :])
    k_fwd = k[:, :HY_W] * window
    k_bwd = k[:, HY_W:] * window
    l1 = jnp.sum(jnp.abs(k_fwd), axis=0) + jnp.sum(jnp.abs(k_bwd[1:]), axis=0)
    return k_fwd / l1, k_bwd / l1


def hyena_mix(p, n_rows, conv_w, conv_b, k_fwd, k_bwd, d_bias):
    u = short_conv3(p, conv_w, n_rows) + conv_b
    x0, x1, v = jnp.split(u, 3, axis=-1)
    z = x1 * v
    L = z.shape[1]
    k_full = jnp.concatenate([k_fwd, jnp.zeros((1, HY_W), F32), k_bwd[:0:-1]], axis=0)
    zf = jnp.fft.rfft(z.astype(F32), n=2 * L, axis=1)
    kf = jnp.fft.rfft(k_full, axis=0)
    y = jnp.fft.irfft(zf * kf[None], n=2 * L, axis=1)[:, :L].astype(z.dtype)
    return (y + z * d_bias) * x0


def shortconv_mix(p, n_rows, conv_w):
    bg, cg, hs = jnp.split(p, 3, axis=-1)
    return bg * short_conv3(cg * hs, conv_w, n_rows)


def _to_chunks(t, n):
    bsz, L, H = t.shape[:3]
    t = t.reshape((bsz, n, CHUNK, H) + t.shape[3:])
    return jnp.moveaxis(t, (1, 3), (0, 2))


def _from_chunks(o):
    o = jnp.moveaxis(o, (0, 2), (1, 3))
    return o.reshape((o.shape[0], o.shape[1] * o.shape[2]) + o.shape[3:])


def gla_chunked(q, k, v, g, s0):
    n = q.shape[1] // CHUNK
    causal = jnp.tril(jnp.ones((CHUNK, CHUNK), dtype=bool))

    def step(S, inp):
        qc, kc, vc, gc = inp
        b = jnp.cumsum(gc, axis=2)
        b_mid = b[:, :, CHUNK // 2:CHUNK // 2 + 1]
        b_last = b[:, :, -1:]
        a = jnp.einsum('bhik,bhjk->bhij', qc * jnp.exp(b - b_mid), kc * jnp.exp(b_mid - b))
        a = jnp.where(causal, a, 0.0)
        o = jnp.einsum('bhij,bhjv->bhiv', a, vc) + jnp.einsum('bhik,bhkv->bhiv', qc * jnp.exp(b), S)
        S = jnp.exp(b_last[:, :, 0])[..., None] * S + jnp.einsum('bhjk,bhjv->bhkv', kc * jnp.exp(b_last - b), vc)
        return S, o

    S, o = lax.scan(step, s0, (_to_chunks(q, n), _to_chunks(k, n), _to_chunks(v, n), _to_chunks(g, n)))
    return _from_chunks(o), S


def gdn_chunked(q, k, v, beta, g, s0):
    n = q.shape[1] // CHUNK
    V = v.shape[-1]
    tril_incl = jnp.tril(jnp.ones((CHUNK, CHUNK), dtype=bool))
    tril_strict = jnp.tril(jnp.ones((CHUNK, CHUNK), dtype=bool), k=-1)
    eye = jnp.eye(CHUNK, dtype=q.dtype)

    def step(S, inp):
        qc, kc, vc, bc, gc = inp
        cum = jnp.cumsum(gc, axis=-1)
        decay = jnp.exp(jnp.where(tril_incl, cum[..., :, None] - cum[..., None, :], -jnp.inf))
        kb = kc * bc[..., None]
        a_strict = jnp.where(tril_strict, jnp.einsum('bhik,bhjk->bhij', kb, kc) * decay, 0.0)
        rhs = jnp.concatenate([vc * bc[..., None], kb * jnp.exp(cum)[..., None]], axis=-1)
        sol = lax.linalg.triangular_solve(a_strict + eye, rhs, left_side=True, lower=True, unit_diagonal=True)
        u, w = sol[..., :V], sol[..., V:]
        v_new = u - jnp.einsum('bhck,bhkv->bhcv', w, S)
        attn = jnp.einsum('bhik,bhjk->bhij', qc, kc) * decay
        o = jnp.einsum('bhik,bhkv->bhiv', qc * jnp.exp(cum)[..., None], S) + jnp.einsum('bhij,bhjv->bhiv', attn, v_new)
        S = jnp.exp(cum[..., -1])[..., None, None] * S + jnp.einsum(
            'bhjk,bhjv->bhkv', kc * jnp.exp(cum[..., -1:] - cum)[..., None], v_new)
        return S, o

    xs = (_to_chunks(q, n), _to_chunks(k, n), _to_chunks(v, n), _to_chunks(beta, n), _to_chunks(g, n))
    S, o = lax.scan(step, s0, xs)
    return _from_chunks(o), S


def bidirectional_prefix_scan(scan_fn, ctx_f, ctx_b, lat_f, lat_b, s0):
    flip = lambda args: tuple(a[:, ::-1] for a in args)
    oc_f, sc_f = scan_fn(*ctx_f, s0)
    ol_f, _ = scan_fn(*lat_f, sc_f)
    oc_b, sc_b = scan_fn(*flip(ctx_b), s0)
    ol_b, _ = scan_fn(*flip(lat_b), sc_b)
    return oc_f + oc_b[:, ::-1], ol_f + ol_b[:, ::-1]


def gla_mix(pc, pl, w_a, b_a, norm_w):
    def prep(p):
        bsz, L, _ = p.shape
        q, k, v, r, a_f, a_b = _split(p, [GLA_QK, GLA_QK, GLA_V, GLA_V, GLA_RANK, GLA_RANK])
        heads = lambda t, d: t.astype(F32).reshape(bsz, L, GLA_HEADS, d)
        log_decay = lambda a, i: jax.nn.log_sigmoid((a @ w_a[i] + b_a[i]).astype(F32)) / GLA_TAU
        q = heads(q, GLA_DK) * GLA_DK ** -0.5
        k = heads(k, GLA_DK)
        v = heads(v, GLA_DV)
        return (q, k, v, heads(log_decay(a_f, 0), GLA_DK)), (q, k, v, heads(log_decay(a_b, 1), GLA_DK)), r

    fwd_c, bwd_c, r_c = prep(pc)
    fwd_l, bwd_l, r_l = prep(pl)
    s0 = jnp.zeros((pc.shape[0], GLA_HEADS, GLA_DK, GLA_DV), F32)
    o_c, o_l = bidirectional_prefix_scan(gla_chunked, fwd_c, bwd_c, fwd_l, bwd_l, s0)

    def finish(o, r):
        bsz, L = r.shape[:2]
        o = rms_norm(o, norm_w).reshape(bsz, L, GLA_V)
        return (o * jax.nn.silu(r.astype(F32))).astype(r.dtype)

    return finish(o_c, r_c), finish(o_l, r_l)


def gdn_mix(pc, pl, lat_rows, conv_w, a_log, dt_bias, norm_w):
    def prep(p, n_rows):
        bsz, L, _ = p.shape
        qkv, z, a_f, a_b, b_f, b_b = _split(p, [2 * GDN_KW + GDN_VW, GDN_VW, GDN_HEADS, GDN_HEADS, GDN_HEADS, GDN_HEADS])
        qkv = jax.nn.silu(short_conv3(qkv, conv_w, n_rows)).astype(F32)
        q, k, v = _split(qkv, [GDN_KW, GDN_KW, GDN_VW])
        q = l2_normalize(q.reshape(bsz, L, GDN_HEADS, GDN_DK)) * GDN_DK ** -0.5
        k = l2_normalize(k.reshape(bsz, L, GDN_HEADS, GDN_DK))
        v = v.reshape(bsz, L, GDN_HEADS, GDN_DV)

        def gates(a, b, i):
            g = -jnp.exp(a_log[i].astype(F32)) * jax.nn.softplus(a.astype(F32) + dt_bias[i].astype(F32))
            return jax.nn.sigmoid(b.astype(F32)), g

        beta_f, g_f = gates(a_f, b_f, 0)
        beta_b, g_b = gates(a_b, b_b, 1)
        return (q, k, v, beta_f, g_f), (q, k, v, beta_b, g_b), z

    fwd_c, bwd_c, z_c = prep(pc, 1)
    fwd_l, bwd_l, z_l = prep(pl, lat_rows)
    s0 = jnp.zeros((pc.shape[0], GDN_HEADS, GDN_DK, GDN_DV), F32)
    o_c, o_l = bidirectional_prefix_scan(gdn_chunked, fwd_c, bwd_c, fwd_l, bwd_l, s0)

    def finish(o, z):
        bsz, L = z.shape[:2]
        o = rms_norm(o, norm_w) * jax.nn.silu(z.astype(F32).reshape(bsz, L, GDN_HEADS, GDN_DV))
        return o.reshape(bsz, L, GDN_VW).astype(z.dtype)

    return finish(o_c, z_c), finish(o_l, z_l)


def token_mixers(pc, pl, lat_rows, hy_conv_w, hy_conv_b, hy_w1, hy_b1, hy_w2, hy_b2, hy_w3, hy_b3, hy_w4,
                 hy_freq, hy_d, sc_conv_w, gla_w_a, gla_b_a, gla_norm_w, gdn_conv_w, gdn_a_log, gdn_dt_bias,
                 gdn_norm_w):
    hy_c, sc_c, gla_c, gdn_c = _split(pc, [HY_COLS, SC_COLS, GLA_COLS, GDN_COLS])
    hy_l, sc_l, gla_l, gdn_l = _split(pl, [HY_COLS, SC_COLS, GLA_COLS, GDN_COLS])
    kf_c, kb_c = hyena_filters(pc.shape[1], hy_w1, hy_b1, hy_w2, hy_b2, hy_w3, hy_b3, hy_w4, hy_freq)
    kf_l, kb_l = hyena_filters(pl.shape[1], hy_w1, hy_b1, hy_w2, hy_b2, hy_w3, hy_b3, hy_w4, hy_freq)
    ya_c = hyena_mix(hy_c, 1, hy_conv_w, hy_conv_b, kf_c, kb_c, hy_d)
    ya_l = hyena_mix(hy_l, lat_rows, hy_conv_w, hy_conv_b, kf_l, kb_l, hy_d)
    yb_c = shortconv_mix(sc_c, 1, sc_conv_w)
    yb_l = shortconv_mix(sc_l, lat_rows, sc_conv_w)
    yc_c, yc_l = gla_mix(gla_c, gla_l, gla_w_a, gla_b_a, gla_norm_w)
    yd_c, yd_l = gdn_mix(gdn_c, gdn_l, lat_rows, gdn_conv_w, gdn_a_log, gdn_dt_bias, gdn_norm_w)
    return (jnp.concatenate([ya_c, yb_c, yc_c, yd_c], axis=-1),
            jnp.concatenate([ya_l, yb_l, yc_l, yd_l], axis=-1))


def moe_ffn(h, w_router, b_router, w_gate, b_gate, w_up, b_up, w_down, b_down):
    shape = h.shape
    t = h.reshape(-1, shape[-1])
    logits = (t @ w_router + b_router).astype(F32)
    top_logit, top_idx = lax.top_k(logits, TOP_K)
    top_w = jax.nn.softmax(top_logit, axis=-1)
    combine = jnp.einsum('tk,tke->et', top_w, jax.nn.one_hot(top_idx, N_EXPERTS, dtype=F32)).astype(t.dtype)

    def expert(acc, prm):
        wg, bg, wu, bu, wd, bd, cw = prm
        gate = jnp.minimum(t @ wg + bg, SWIGLU_LIMIT)
        up = jnp.clip(t @ wu + bu, -SWIGLU_LIMIT, SWIGLU_LIMIT)
        y = ((up + 1.0) * gate * jax.nn.sigmoid(SWIGLU_ALPHA * gate)) @ wd + bd
        return acc + cw[:, None] * y, None

    out, _ = lax.scan(expert, jnp.zeros_like(t), (w_gate, b_gate, w_up, b_up, w_down, b_down, combine))
    return out.reshape(shape)


def setup_inputs(seed: int = 0) -> dict:
    key = jax.random.key(seed)
    ks = iter(jax.random.split(key, 48))

    def nrm(shape, scale=1.0):
        return jax.random.normal(next(ks), shape, F32) * scale

    L, E, F = DEPTH, N_EXPERTS, D_EXPERT
    dt = jnp.exp(jax.random.uniform(next(ks), (L, 2, GDN_HEADS), F32, math.log(1e-3), math.log(1e-1)))
    a_log = jnp.log(jax.random.uniform(next(ks), (L, 2, GDN_HEADS), F32, 1.0, 16.0))
    return {
        'x': nrm((BATCH, SEQ, D_MODEL)),
        'c': nrm((BATCH, D_MODEL)),
        'ctx': nrm((BATCH, CTX_LEN, D_MODEL)),
        'c_ctx': nrm((D_MODEL,)),
        'w_ada': nrm((L, D_MODEL, 6 * D_MODEL), D_MODEL ** -0.5),
        'b_ada': nrm((L, 6 * D_MODEL), 0.02),
        'w_in': nrm((L, D_MODEL, D_IN), D_MODEL ** -0.5),
        'hy_conv_w': nrm((L, 3, HY_COLS), 3 ** -0.5),
        'hy_conv_b': nrm((L, HY_COLS), 0.02),
        'hy_w1': nrm((L, HY_EMB, HY_HID), HY_EMB ** -0.5),
        'hy_b1': nrm((L, HY_HID), 0.02),
        'hy_w2': nrm((L, HY_HID, HY_HID), HY_HID ** -0.5),
        'hy_b2': nrm((L, HY_HID), 0.02),
        'hy_w3': nrm((L, HY_HID, HY_HID), HY_HID ** -0.5),
        'hy_b3': nrm((L, HY_HID), 0.02),
        'hy_w4': nrm((L, HY_HID, 2 * HY_W), HY_HID ** -0.5),
        'hy_freq': 1.0 + nrm((L, 3, HY_HID), 0.1),
        'hy_d': nrm((L, HY_W)),
        'sc_conv_w': nrm((L, 3, SC_W), 3 ** -0.5),
        'gla_w_a': nrm((L, 2, GLA_RANK, GLA_QK), GLA_RANK ** -0.5),
        'gla_b_a': nrm((L, 2, GLA_QK), 0.02),
        'gla_norm_w': 1.0 + nrm((L, GLA_DV), 0.02),
        'gdn_conv_w': nrm((L, 3, 2 * GDN_KW + GDN_VW), 3 ** -0.5),
        'gdn_a_log': a_log,
        'gdn_dt_bias': dt + jnp.log(-jnp.expm1(-dt)),
        'gdn_norm_w': 1.0 + nrm((L, GDN_DV), 0.02),
        'w_out': nrm((L, D_MIX, D_MODEL), D_MIX ** -0.5 * DN_BETA),
        'ln1_g': 1.0 + nrm((L, D_MODEL), 0.02),
        'ln1_b': nrm((L, D_MODEL), 0.02),
        'w_router': nrm((L, D_MODEL, E), D_MODEL ** -0.5),
        'b_router': nrm((L, E), 0.01),
        'w_gate': nrm((L, E, D_MODEL, F), D_MODEL ** -0.5),
        'b_gate': nrm((L, E, F), 0.02),
        'w_up': nrm((L, E, D_MODEL, F), D_MODEL ** -0.5),
        'b_up': nrm((L, E, F), 0.02),
        'w_down': nrm((L, E, F, D_MODEL), F ** -0.5 * DN_BETA),
        'b_down': nrm((L, E, D_MODEL), 0.02),
        'ln2_g': 1.0 + nrm((L, D_MODEL), 0.02),
        'ln2_b': nrm((L, D_MODEL), 0.02),
    }


def reference(x, c, ctx, c_ctx, w_ada, b_ada, w_in, hy_conv_w, hy_conv_b, hy_w1, hy_b1, hy_w2, hy_b2, hy_w3,
              hy_b3, hy_w4, hy_freq, hy_d, sc_conv_w, gla_w_a, gla_b_a, gla_norm_w, gdn_conv_w, gdn_a_log,
              gdn_dt_bias, gdn_norm_w, w_out, ln1_g, ln1_b, w_router, b_router, w_gate, b_gate, w_up, b_up,
              w_down, b_down, ln2_g, ln2_b):
    ROWS = x.shape[1] // GRID_W
    n_ctx = ctx.shape[1]
    silu_c = jax.nn.silu(c)
    silu_cc = jax.nn.silu(c_ctx)
    xl, xc = x, ctx
    for l in range(DEPTH):
        last = l == DEPTH - 1
        sh1_l, sc1_l, g1_l, sh2_l, sc2_l, g2_l = jnp.split((silu_c @ w_ada[l] + b_ada[l])[:, None, :], 6, axis=-1)
        sh1_c, sc1_c, g1_c, sh2_c, sc2_c, g2_c = jnp.split((silu_cc @ w_ada[l] + b_ada[l])[None, None, :], 6, axis=-1)
        pl = (xl * (1.0 + sc1_l) + sh1_l) @ w_in[l]
        pc = (xc * (1.0 + sc1_c) + sh1_c) @ w_in[l]
        yc, yl = token_mixers(pc, pl, ROWS, hy_conv_w[l], hy_conv_b[l], hy_w1[l], hy_b1[l], hy_w2[l], hy_b2[l],
                              hy_w3[l], hy_b3[l], hy_w4[l], hy_freq[l], hy_d[l], sc_conv_w[l], gla_w_a[l],
                              gla_b_a[l], gla_norm_w[l], gdn_conv_w[l], gdn_a_log[l], gdn_dt_bias[l], gdn_norm_w[l])
        xl = layer_norm(DN_ALPHA * xl + g1_l * (yl @ w_out[l]), ln1_g[l], ln1_b[l])
        hl = xl * (1.0 + sc2_l) + sh2_l
        if last:
            fl = moe_ffn(hl, w_router[l], b_router[l], w_gate[l], b_gate[l], w_up[l], b_up[l], w_down[l], b_down[l])
        else:
            xc = layer_norm(DN_ALPHA * xc + g1_c * (yc @ w_out[l]), ln1_g[l], ln1_b[l])
            hc = xc * (1.0 + sc2_c) + sh2_c
            f = moe_ffn(jnp.concatenate([hc, hl], axis=1), w_router[l], b_router[l], w_gate[l], b_gate[l],
                        w_up[l], b_up[l], w_down[l], b_down[l])
            fc, fl = f[:, :n_ctx], f[:, n_ctx:]
            xc = layer_norm(DN_ALPHA * xc + g2_c * fc, ln2_g[l], ln2_b[l])
        xl = layer_norm(DN_ALPHA * xl + g2_l * fl, ln2_g[l], ln2_b[l])
    return xl
```

```python
import functools
import math

import numpy as np
import jax
import jax.numpy as jnp
from jax import lax
from jax.experimental import pallas as pl
from jax.experimental.pallas import tpu as pltpu

F32 = jnp.float32
BF16 = jnp.bfloat16
I32 = jnp.int32
HI = lax.Precision.HIGHEST

D = 1024
GW = 256
CHUNK = 64
ROW_W = 64
GLA_DK = 32
GLA_TAU = 16.0
GDN_DK = 64
N_EXP = 32
TOP_K = 4
SWIGLU_LIMIT = 7.0
SWIGLU_ALPHA = 1.702
LN_EPS = 1e-5
RMS_EPS = 1e-6
L2_EPS = 1e-6
HY_EMB = 33
HY_HID = 64
NEG_BIG = -1e30
P_HY, P_SC, P_GLA, P_GDN = 768, 768, 896, 1152
P_IN = P_HY + P_SC + P_GLA + P_GDN
VMEM_LIMIT = 56 * 1024 * 1024


def _cparams(sem):
    return pltpu.CompilerParams(dimension_semantics=sem, vmem_limit_bytes=VMEM_LIMIT)


def _split_bf16(x):
    hi = x.astype(BF16)
    lo = (x - hi.astype(F32)).astype(BF16)
    return hi, lo


def _dot(a, b):
    return jnp.dot(a, b, preferred_element_type=F32)


def _dot_nt(a, b):
    return lax.dot_general(a, b, (((1,), (1,)), ((), ())), preferred_element_type=F32)


def _dot_tn(a, b):
    return lax.dot_general(a, b, (((0,), (0,)), ((), ())), preferred_element_type=F32)


def _dot_sel(x, sel_bf16):
    h1 = x.astype(BF16)
    r1 = x - h1.astype(F32)
    h2 = r1.astype(BF16)
    h3 = (r1 - h2.astype(F32)).astype(BF16)
    return _dot(h1, sel_bf16) + _dot(h2, sel_bf16) + _dot(h3, sel_bf16)


def _sel_dot(sel_bf16, x):
    h1 = x.astype(BF16)
    r1 = x - h1.astype(F32)
    h2 = r1.astype(BF16)
    h3 = (r1 - h2.astype(F32)).astype(BF16)
    return _dot(sel_bf16, h1) + _dot(sel_bf16, h2) + _dot(sel_bf16, h3)


def _iota(shape, dim):
    return lax.broadcasted_iota(I32, shape, dim)


def _sigmoid(x):
    return 1.0 / (1.0 + jnp.exp(-x))


def _silu(x):
    return x * _sigmoid(x)


def _softplus(x):
    return jnp.maximum(x, 0.0) + jnp.log(1.0 + jnp.exp(-jnp.abs(x)))


def _log_sigmoid(x):
    return -_softplus(-x)


def _ada_kernel(cc_ref, w_ref, b_ref, o_ref):
    o_ref[0] = jnp.dot(_silu(cc_ref[...]), w_ref[0], precision=HI, preferred_element_type=F32) + b_ref[0]


def _ada_mods(cc, w_ada, b_ada):
    depth, _, n6 = w_ada.shape
    rows = cc.shape[0]
    tn = 1536
    return pl.pallas_call(
        _ada_kernel,
        grid=(depth, n6 // tn),
        in_specs=[pl.BlockSpec((rows, D), lambda l, j: (0, 0)),
                  pl.BlockSpec((1, D, tn), lambda l, j: (l, 0, j)),
                  pl.BlockSpec((1, 1, tn), lambda l, j: (l, 0, j))],
        out_specs=pl.BlockSpec((1, rows, tn), lambda l, j: (l, 0, j)),
        out_shape=jax.ShapeDtypeStruct((depth, rows, n6), F32),
        compiler_params=_cparams(("arbitrary", "arbitrary")),
        name="ada_mods",
    )(cc, w_ada, b_ada.reshape(depth, 1, n6))


def _conv3(x, w, rowlen):
    n = x.shape[0]
    pos = _iota(x.shape, 0) & (rowlen - 1)
    xm = jnp.where(pos == 0, 0.0, pltpu.roll(x, 1, axis=0))
    xp = jnp.where(pos == rowlen - 1, 0.0, pltpu.roll(x, n - 1, axis=0))
    return xm * w[0:1] + x * w[1:2] + xp * w[2:3]


def _inproj_kernel(x_ref, scp_ref, sh_ref, w_ref, hyw_ref, hyb_ref, hyd_ref, scw_ref, glaw_ref, glab_ref,
                   gdnw_ref, gdna_ref, gdnd_ref, ones_ref,
                   zb_ref, e_ref, x0_ref, ysc_ref, gqk_ref, gv_ref, gr_ref, gg_ref,
                   dq_ref, dk_ref, dv_ref, dz_ref, dg_ref, *, n_lat_tiles, ctx_len):
    i = pl.program_id(1)
    rowlen = jnp.where(i == n_lat_tiles, ctx_len, ROW_W)
    xm = (x_ref[0] * scp_ref[0] + sh_ref[0]).astype(BF16)

    u = _conv3(_dot(xm, w_ref[:, 0:P_HY]), hyw_ref[...], rowlen) + hyb_ref[...]
    z = u[:, GW:2 * GW] * u[:, 2 * GW:3 * GW]
    zb_ref[0] = z.astype(BF16)
    e_ref[0] = z * hyd_ref[...]
    x0_ref[0] = u[:, 0:GW]

    ps = _dot(xm, w_ref[:, P_HY:P_HY + P_SC])
    ysc_ref[0] = ps[:, 0:GW] * _conv3(ps[:, GW:2 * GW] * ps[:, 2 * GW:3 * GW], scw_ref[...], rowlen)

    pg = _dot(xm, w_ref[:, P_HY + P_SC:P_HY + P_SC + P_GLA])
    gqk_ref[0] = pg[:, 0:256]
    gv_ref[0] = pg[:, 256:512]
    gr_ref[0] = pg[:, 512:768]
    ah, al = _split_bf16(pg[:, 768:896])
    wh, wl = _split_bf16(glaw_ref[...])
    logit = _dot(ah, wh) + _dot(al, wh) + _dot(ah, wl) + glab_ref[...]
    gg_ref[0] = _log_sigmoid(logit) * (1.0 / GLA_TAU)

    o3 = P_HY + P_SC + P_GLA
    pd = _dot(xm, w_ref[:, o3:o3 + P_GDN])
    qkv = _silu(_conv3(pd[:, 0:768], gdnw_ref[...], rowlen))
    ones_bd = ones_ref[...]
    q = qkv[:, 0:256]
    k = qkv[:, 256:512]
    dq_ref[0] = q * lax.rsqrt(_dot_sel(q * q, ones_bd) + L2_EPS) * (GDN_DK ** -0.5)
    dk_ref[0] = k * lax.rsqrt(_dot_sel(k * k, ones_bd) + L2_EPS)
    dv_ref[0] = qkv[:, 512:768]
    dz_ref[0] = pd[:, 768:1024]
    gl = pd[:, 1024:1152]
    lane = _iota(gl.shape, 1)
    dg_ref[0] = jnp.where(lane < 8, gdna_ref[...] * _softplus(gl + gdnd_ref[...]), _sigmoid(gl))


def _inproj(xcat, scp, sh, w_bf16, hyw, hyb, hyd, scw, glaw, glab, gdnw, gdna, gdnd, ones_bd, *, NL, NC):
    B, TT, _ = xcat.shape
    TR = NC
    n_lat_tiles = NL // TR
    nt = TT // TR
    mod_idx = lambda b, i: (jnp.where(i == n_lat_tiles, B, b), 0, 0)
    const2 = lambda b, i: (0, 0)
    row_blk = lambda c: pl.BlockSpec((1, TR, c), lambda b, i: (b, i, 0))
    outs = [(GW, BF16)] + [(GW, F32)] * 11 + [(128, F32)]
    return pl.pallas_call(
        functools.partial(_inproj_kernel, n_lat_tiles=n_lat_tiles, ctx_len=NC),
        grid=(B, nt),
        in_specs=[row_blk(D),
                  pl.BlockSpec((1, 1, D), mod_idx), pl.BlockSpec((1, 1, D), mod_idx),
                  pl.BlockSpec((D, P_IN), const2),
                  pl.BlockSpec((3, P_HY), const2), pl.BlockSpec((1, P_HY), const2), pl.BlockSpec((1, GW), const2),
                  pl.BlockSpec((3, GW), const2),
                  pl.BlockSpec((128, 256), const2), pl.BlockSpec((1, 256), const2),
                  pl.BlockSpec((3, 768), const2), pl.BlockSpec((1, 128), const2), pl.BlockSpec((1, 128), const2),
                  pl.BlockSpec((256, 256), const2)],
        out_specs=[row_blk(c) for c, _ in outs],
        out_shape=[jax.ShapeDtypeStruct((B, TT, c), dt) for c, dt in outs],
        compiler_params=_cparams(("arbitrary", "arbitrary")),
        name="inproj_local",
    )(xcat, scp, sh, w_bf16, hyw, hyb, hyd, scw, glaw, glab, gdnw, gdna, gdnd, ones_bd)


def _dft_tables(L):
    N = 2 * L
    f = jnp.arange(L, dtype=I32)[:, None]
    t = jnp.arange(L, dtype=I32)[None, :]
    ang = ((f * t) % N).astype(F32) * (2.0 * math.pi / N)
    cos = jnp.cos(ang)
    sin = jnp.sin(ang)
    nyq = jnp.where(t % 2 == 0, 1.0, -1.0).astype(F32)
    mc = cos
    ms = jnp.where(f == 0, nyq, -sin)
    wf = jnp.where(f == 0, 1.0, 2.0).astype(F32) / N
    ic = (cos * wf).T
    isn = jnp.where(f == 0, nyq / N, -sin * wf).T
    return mc, ms, ic, isn


def _hyfilt_kernel(zf_ref, win_ref, w1_ref, b1_ref, w2_ref, b2_ref, w3_ref, b3_ref, w4_ref, fr_ref,
                   mc_ref, ms_ref, o_ref, ks_ref):
    j = pl.program_id(1)

    @pl.when(j == 0)
    def _():
        dot = lambda a, b: jnp.dot(a, b, precision=HI, preferred_element_type=F32)
        h = jnp.sin(fr_ref[0, 0:1] * (dot(zf_ref[...], w1_ref[0]) + b1_ref[0]))
        h = jnp.sin(fr_ref[0, 1:2] * (dot(h, w2_ref[0]) + b2_ref[0]))
        h = jnp.sin(fr_ref[0, 2:3] * (dot(h, w3_ref[0]) + b3_ref[0]))
        k = dot(h, w4_ref[0])
        win = win_ref[...]
        kf = k[:, 0:GW] * win
        kb = k[:, GW:2 * GW] * win
        kb = jnp.where(_iota(kb.shape, 0) == 0, 0.0, kb)
        l1 = jnp.sum(jnp.abs(kf), axis=0, keepdims=True) + jnp.sum(jnp.abs(kb), axis=0, keepdims=True)
        kf = kf / l1
        kb = kb / l1
        ks_ref[:, 0:GW] = kf + kb
        ks_ref[:, GW:2 * GW] = kf - kb

    ks = ks_ref[...]
    o_ref[0, 0] = jnp.dot(mc_ref[...], ks, precision=HI, preferred_element_type=F32)
    o_ref[0, 1] = jnp.dot(ms_ref[...], ks, precision=HI, preferred_element_type=F32)


def _hyena_spectrum(L, mc, ms, w1p, b1, w2, b2, w3, b3, w4, freq):
    depth = w1p.shape[0]
    t = jnp.linspace(0.0, 1.0, L, dtype=F32)[:, None]
    bands = (HY_EMB - 1) // 2
    ang = 2.0 * math.pi * jnp.arange(L, dtype=F32)[:, None] / L
    f = jnp.linspace(1e-4, bands - 1, bands, dtype=F32)[None, :]
    zf = jnp.concatenate([t, jnp.cos(f * ang), -jnp.sin(f * ang), jnp.zeros((L, HY_HID - HY_EMB), F32)], axis=-1)
    max_decay = math.log(1e-2) / 0.3
    min_decay = math.log(1e-2) / 1.5
    deltas = jnp.abs(jnp.linspace(min_decay, max_decay, GW, dtype=F32))
    win = jnp.exp(-t * deltas[None, :])
    ft = min(L, 512)
    c2 = lambda l, j: (0, 0)
    lw = lambda *s: pl.BlockSpec((1,) + s, lambda l, j: (l,) + (0,) * len(s))
    a = pl.pallas_call(
        _hyfilt_kernel,
        grid=(depth, L // ft),
        in_specs=[pl.BlockSpec((L, HY_HID), c2), pl.BlockSpec((L, GW), c2),
                  lw(HY_HID, HY_HID), lw(1, HY_HID), lw(HY_HID, HY_HID), lw(1, HY_HID),
                  lw(HY_HID, HY_HID), lw(1, HY_HID), lw(HY_HID, 2 * GW), lw(3, HY_HID),
                  pl.BlockSpec((ft, L), lambda l, j: (j, 0)), pl.BlockSpec((ft, L), lambda l, j: (j, 0))],
        out_specs=pl.BlockSpec((1, 2, ft, 2 * GW), lambda l, j: (l, 0, j, 0)),
        out_shape=jax.ShapeDtypeStruct((depth, 2, L, 2 * GW), F32),
        scratch_shapes=[pltpu.VMEM((L, 2 * GW), F32)],
        compiler_params=_cparams(("arbitrary", "arbitrary")),
        name="hyena_filter_spectrum",
    )(zf, win, w1p, b1, w2, b2, w3, b3, w4, freq, mc, ms)
    k_re = a[:, 0, :, 0:GW]
    k_im = a[:, 1, :, GW:2 * GW]
    k_im = k_im.at[:, 0, :].set(a[:, 1, 0, 0:GW])
    return jnp.stack([k_re, k_im], axis=1)


def _hyconv_kernel(zb_ref, e_ref, x0_ref, k_ref, mc_ref, ms_ref, ic_ref, is_ref, o_ref, acc_ref, *, G):
    j = pl.program_id(1)
    nj = pl.num_programs(1)
    kt = k_ref[0, 0]
    kb = k_ref[0, 1]
    row0 = jnp.logical_and(_iota(kt.shape, 0) == 0, j == 0)
    mc = mc_ref[...]
    ms = ms_ref[...]
    ic = ic_ref[...]
    isn = is_ref[...]
    for g in range(G):
        z = zb_ref[g]
        zt = _dot(mc, z)
        zi = _dot(ms, z)
        yt = zt * kt - jnp.where(row0, 0.0, zi * kb)
        yi = jnp.where(row0, zi * kb, zt * kb + zi * kt)
        contrib = _dot(ic, yt.astype(BF16)) + _dot(isn, yi.astype(BF16))

        @pl.when(j == 0)
        def _():
            acc_ref[g] = contrib

        @pl.when(j > 0)
        def _():
            acc_ref[g] += contrib

    @pl.when(j == nj - 1)
    def _():
        for g in range(G):
            o_ref[g] = (acc_ref[g] + e_ref[g]) * x0_ref[g]


def _hyconv(zb, e, x0, kspec_l, tabs, *, L, row_blk):
    B = zb.shape[0]
    G = 2 if B % 2 == 0 else 1
    ft = min(L, 256)
    mc, ms, ic, isn = tabs
    seg = lambda: pl.BlockSpec((G, L, GW), lambda b, j: (b, row_blk, 0))
    return pl.pallas_call(
        functools.partial(_hyconv_kernel, G=G),
        grid=(B // G, L // ft),
        in_specs=[seg(), seg(), seg(),
                  pl.BlockSpec((1, 2, ft, GW), lambda b, j: (0, 0, j, 0)),
                  pl.BlockSpec((ft, L), lambda b, j: (j, 0)), pl.BlockSpec((ft, L), lambda b, j: (j, 0)),
                  pl.BlockSpec((L, ft), lambda b, j: (0, j)), pl.BlockSpec((L, ft), lambda b, j: (0, j))],
        out_specs=pl.BlockSpec((G, L, GW), lambda b, j: (b, 0, 0)),
        out_shape=jax.ShapeDtypeStruct((B, L, GW), F32),
        scratch_shapes=[pltpu.VMEM((G, L, GW), F32)],
        compiler_params=_cparams(("arbitrary", "arbitrary")),
        name=f"hyena_longconv_{L}",
    )(zb, e, x0, kspec_l, mc, ms, ic, isn)


def _pad_cols(a, n):
    return jnp.pad(a, [(0, 0)] * (a.ndim - 1) + [(0, n - a.shape[-1])])


def _prep_inproj_params(w_in, hy_conv_w, hy_conv_b, hy_d, sc_conv_w, gla_w_a, gla_b_a, gdn_conv_w, gdn_a_log,
                        gdn_dt_bias):
    depth = w_in.shape[0]
    o1, o2, o3 = 768, 1536, 1536 + 800
    w = jnp.concatenate([w_in[..., 0:o2], _pad_cols(w_in[..., o2:o3], P_GLA), _pad_cols(w_in[..., o3:], P_GDN)],
                        axis=-1).astype(BF16)
    glaw = jnp.zeros((depth, 128, 256), F32)
    glaw = glaw.at[:, 0:16, 0:128].set(gla_w_a[:, 0]).at[:, 16:32, 128:256].set(gla_w_a[:, 1])
    glab = gla_b_a.reshape(depth, 1, 256)
    gdna = _pad_cols(-jnp.exp(gdn_a_log.astype(F32)).reshape(depth, 1, 8), 128)
    gdnd = _pad_cols(gdn_dt_bias.astype(F32).reshape(depth, 1, 8), 128)
    return dict(w=w, hyw=hy_conv_w, hyb=hy_conv_b[:, None, :], hyd=hy_d[:, None, :], scw=sc_conv_w, glaw=glaw,
                glab=glab, gdnw=gdn_conv_w, gdna=gdna, gdnd=gdnd)


def _block_ones(n, blk):
    r = np.arange(n)
    return jnp.asarray((r[:, None] // blk) == (r[None, :] // blk), BF16)


def _tri_pair():
    r = _iota((CHUNK, CHUNK), 0)
    c = _iota((CHUNK, CHUNK), 1)
    return [(r >= c).astype(BF16), (r <= c).astype(BF16)]


def _cat_masks():
    r = _iota((CHUNK, 4 * CHUNK), 0)
    c = _iota((CHUNK, 4 * CHUNK), 1) & (CHUNK - 1)
    return r, c


def _scan_loops(NL, NC, chunk_fn, fwd_store, bwd_finish, reset):
    n_c, n_l = NC // CHUNK, NL // CHUNK

    def run(n, base0, d, reverse, sink):
        def body(i, carry):
            c = (n - 1 - i) if reverse else i
            base = pl.multiple_of(base0 + c * CHUNK, CHUNK)
            sink(base, chunk_fn(base, d))
            return carry
        lax.fori_loop(0, n, body, 0)

    reset()
    run(n_c, NL, 0, False, fwd_store)
    run(n_l, 0, 0, False, fwd_store)
    reset()
    run(n_c, NL, 1, True, bwd_finish)
    run(n_l, 0, 1, True, bwd_finish)


def _gla_kernel(qk_ref, v_ref, r_ref, g_ref, nw_ref, ones_ref, o_ref, of_ref, st_ref, *, NL, NC):
    C = CHUNK
    scale = GLA_DK ** -0.5
    tri = _tri_pair()
    ri, ci = _cat_masks()
    cmask = [ci <= ri, ci >= ri]
    hm = (_iota((256, 128), 0) >> 6) == (_iota((256, 128), 1) >> 5)
    vbd = (_iota((256, 256), 0) >> 6) == (_iota((256, 256), 1) >> 6)

    def chunk(base, d):
        rows = pl.ds(base, C)
        q = qk_ref[0, rows, 0:128] * scale
        k = qk_ref[0, rows, 128:256]
        v = v_ref[0, rows, :]
        g = g_ref[0, rows, d * 128:(d + 1) * 128]
        b = _sel_dot(tri[d], g)
        mid = C // 2 if d == 0 else C // 2 - 1
        last = C - 1 if d == 0 else 0
        b_mid = b[mid:mid + 1]
        b_last = b[last:last + 1]
        qe = (q * jnp.exp(b - b_mid)).astype(BF16)
        ke = (k * jnp.exp(b_mid - b)).astype(BF16)
        kbd = jnp.where(hm, jnp.concatenate([ke] * 4, axis=0), jnp.zeros((), BF16))
        a = jnp.where(cmask[d], _dot_nt(qe, kbd), 0.0)
        vb = jnp.where(vbd, jnp.concatenate([v.astype(BF16)] * 4, axis=0), jnp.zeros((), BF16))
        st = st_ref[...]
        o = _dot(a.astype(BF16), vb) + _dot_nt((q * jnp.exp(b)).astype(BF16), st.astype(BF16))
        kd = (k * jnp.exp(b_last - b)).astype(BF16)
        st_ref[...] = st * jnp.exp(b_last) + jnp.where(hm, _dot_tn(v.astype(BF16), kd), 0.0)
        return o

    def fwd_store(base, o):
        of_ref[pl.ds(base, C), :] = o

    def bwd_finish(base, ob):
        rows = pl.ds(base, C)
        o = of_ref[rows, :] + ob
        ms = _dot_sel(o * o, ones_ref[...]) * (1.0 / 64.0)
        o_ref[0, rows, :] = o * lax.rsqrt(ms + RMS_EPS) * nw_ref[...] * _silu(r_ref[0, rows, :])

    def reset():
        st_ref[...] = jnp.zeros(st_ref.shape, F32)

    _scan_loops(NL, NC, chunk, fwd_store, bwd_finish, reset)


def _gla_scan(gqk, gv, gr, gg, nw, ones_bd, *, NL, NC):
    B, TT, _ = gqk.shape
    blk = lambda: pl.BlockSpec((1, TT, 256), lambda b: (b, 0, 0))
    return pl.pallas_call(
        functools.partial(_gla_kernel, NL=NL, NC=NC),
        grid=(B,),
        in_specs=[blk(), blk(), blk(), blk(), pl.BlockSpec((1, 256), lambda b: (0, 0)),
                  pl.BlockSpec((256, 256), lambda b: (0, 0))],
        out_specs=blk(),
        out_shape=jax.ShapeDtypeStruct((B, TT, 256), F32),
        scratch_shapes=[pltpu.VMEM((TT, 256), F32), pltpu.VMEM((256, 128), F32)],
        compiler_params=_cparams(("arbitrary",)),
        name="gla_scan",
    )(gqk, gv, gr, gg, nw, ones_bd)


def _gdn_kernel(q_ref, k_ref, v_ref, z_ref, g_ref, nw_ref, ones_ref, o_ref, of_ref, s_ref, *, NL, NC):
    C = CHUNK
    tri = _tri_pair()
    ri, ci = _cat_masks()
    m_incl = [ci <= ri, ci >= ri]
    m_strict = [ci < ri, ci > ri]
    dmask = ci == ri
    eye_cat = dmask.astype(F32)
    bd = (_iota((256, 256), 0) >> 6) == (_iota((256, 256), 1) >> 6)
    ones64 = jnp.ones((C, C), BF16)
    lane_r = _iota((128, 256), 0)
    head_c = _iota((128, 256), 1) >> 6
    eg = [(lane_r == head_c + 4 * d).astype(BF16) for d in range(2)]
    eb = [(lane_r == head_c + 8 + 4 * d).astype(BF16) for d in range(2)]

    def block_diag(y):
        return jnp.where(bd, jnp.concatenate([y.astype(BF16)] * 4, axis=0), jnp.zeros((), BF16))

    def mm_cat(x, y):
        return _dot(x.astype(BF16), block_diag(y))

    def pair_mask(ls):
        return jnp.logical_and((ri >> (ls + 1)) == (ci >> (ls + 1)), (ri >> ls) != (ci >> ls))

    def chunk(base, d):
        rows = pl.ds(base, C)
        q = q_ref[0, rows, :]
        k = k_ref[0, rows, :]
        v = v_ref[0, rows, :]
        gt = g_ref[0, rows, :]
        last = C - 1 if d == 0 else 0
        cexp = _dot_sel(_sel_dot(tri[d], gt), eg[d])
        ct = _sel_dot(ones64, jnp.where(dmask, cexp, 0.0))
        decay = jnp.exp(jnp.where(m_incl[d], cexp - ct, -jnp.inf))
        bexp = _dot_sel(gt, eb[d])
        kb = k * bexp
        kbd = block_diag(k)
        a = jnp.where(m_strict[d], _dot_nt(kb.astype(BF16), kbd) * decay, 0.0)
        t = eye_cat - jnp.where(pair_mask(0), a, 0.0)
        for ls in range(1, 6):
            t = t - mm_cat(mm_cat(t, jnp.where(pair_mask(ls), a, 0.0)), t)
        ecum = jnp.exp(cexp)
        u = mm_cat(t, v * bexp)
        w = mm_cat(t, kb * ecum)
        s = s_ref[...]
        sb = s.astype(BF16)
        v_new = u - _dot(w.astype(BF16), sb)
        attn = jnp.where(m_incl[d], _dot_nt(q.astype(BF16), kbd) * decay, 0.0)
        o = _dot((q * ecum).astype(BF16), sb) + mm_cat(attn, v_new)
        cl = cexp[last:last + 1]
        kd = (k * jnp.exp(cl - cexp)).astype(BF16)
        s_ref[...] = s * jnp.exp(cl) + jnp.where(bd, _dot_tn(kd, v_new.astype(BF16)), 0.0)
        return o

    def fwd_store(base, o):
        of_ref[pl.ds(base, C), :] = o

    def bwd_finish(base, ob):
        rows = pl.ds(base, C)
        o = of_ref[rows, :] + ob
        ms = _dot_sel(o * o, ones_ref[...]) * (1.0 / 64.0)
        o_ref[0, rows, :] = o * lax.rsqrt(ms + RMS_EPS) * nw_ref[...] * _silu(z_ref[0, rows, :])

    def reset():
        s_ref[...] = jnp.zeros(s_ref.shape, F32)

    _scan_loops(NL, NC, chunk, fwd_store, bwd_finish, reset)


def _layer_norm(t, g, b):
    tc = t - jnp.mean(t, axis=-1, keepdims=True)
    var = jnp.mean(tc * tc, axis=-1, keepdims=True)
    return tc * lax.rsqrt(var + LN_EPS) * g + b


def _outproj_kernel(yal_ref, yac_ref, yb_ref, yc_ref, yd_ref, x_ref, g1_ref, scp_ref, sh_ref, w_ref, lg_ref, lb_ref,
                    wr_ref, br_ref, x1_ref, h_ref, ti_ref, tw_ref, *, alpha, n_lat_tiles):
    ya = jnp.where(pl.program_id(1) == n_lat_tiles, yac_ref[0], yal_ref[0])
    mix = (_dot(ya.astype(BF16), w_ref[0:GW]) + _dot(yb_ref[0].astype(BF16), w_ref[GW:2 * GW])
           + _dot(yc_ref[0].astype(BF16), w_ref[2 * GW:3 * GW]) + _dot(yd_ref[0].astype(BF16), w_ref[3 * GW:4 * GW]))
    x1 = _layer_norm(alpha * x_ref[0] + g1_ref[0] * mix, lg_ref[...], lb_ref[...])
    x1_ref[0] = x1
    h = x1 * scp_ref[0] + sh_ref[0]
    h_ref[0] = h
    logits = jnp.dot(h, wr_ref[...], precision=HI, preferred_element_type=F32) + br_ref[...]
    lane = _iota(logits.shape, 1)
    vals, idxs = [], []
    for _ in range(TOP_K):
        m = jnp.max(logits, axis=-1, keepdims=True)
        idx = jnp.min(jnp.where(logits == m, lane, 128), axis=-1, keepdims=True)
        vals.append(m)
        idxs.append(idx)
        logits = jnp.where(lane == idx, -jnp.inf, logits)
    ex = [jnp.exp(v - vals[0]) for v in vals]
    inv = 1.0 / (ex[0] + ex[1] + ex[2] + ex[3])
    ti = jnp.zeros(lane.shape, I32)
    tw = jnp.zeros(lane.shape, F32)
    for kk in range(TOP_K):
        ti = jnp.where(lane == kk, idxs[kk], ti)
        tw = jnp.where(lane == kk, ex[kk] * inv, tw)
    ti_ref[0] = ti
    tw_ref[0] = tw


def _outproj(ya_lat, ya_ctx, yb, yc, yd, xcat, g1, scp, sh, w_bf16, lg, lb, wr, br, *, NL, NC, alpha):
    B, TT, _ = xcat.shape
    TR = NC
    n_lat_tiles = NL // TR
    mod_idx = lambda b, i: (jnp.where(i == n_lat_tiles, B, b), 0, 0)
    const2 = lambda b, i: (0, 0)
    row_blk = lambda c: pl.BlockSpec((1, TR, c), lambda b, i: (b, i, 0))
    mod = lambda: pl.BlockSpec((1, 1, D), mod_idx)
    return pl.pallas_call(
        functools.partial(_outproj_kernel, alpha=alpha, n_lat_tiles=n_lat_tiles),
        grid=(B, TT // TR),
        in_specs=[pl.BlockSpec((1, TR, GW), lambda b, i: (b, jnp.minimum(i, n_lat_tiles - 1), 0)),
                  pl.BlockSpec((1, TR, GW), lambda b, i: (b, 0, 0)),
                  row_blk(GW), row_blk(GW), row_blk(GW), row_blk(D), mod(), mod(), mod(),
                  pl.BlockSpec((D, D), const2), pl.BlockSpec((1, D), const2), pl.BlockSpec((1, D), const2),
                  pl.BlockSpec((D, 128), const2), pl.BlockSpec((1, 128), const2)],
        out_specs=[row_blk(D), row_blk(D), row_blk(128), row_blk(128)],
        out_shape=[jax.ShapeDtypeStruct((B, TT, D), F32), jax.ShapeDtypeStruct((B, TT, D), F32),
                   jax.ShapeDtypeStruct((B, TT, 128), I32), jax.ShapeDtypeStruct((B, TT, 128), F32)],
        compiler_params=_cparams(("arbitrary", "arbitrary")),
        name="outproj_ln_router",
    )(ya_lat, ya_ctx, yb, yc, yd, xcat, g1, scp, sh, w_bf16, lg, lb, wr, br)


def _route(tidx, tm):
    T = tidx.shape[0]
    A = T * TOP_K
    e = tidx.reshape(A)
    onehot = (e[:, None] == jnp.arange(N_EXP, dtype=I32)[None, :]).astype(I32)
    csum = jnp.cumsum(onehot, axis=0)
    counts = csum[-1]
    rank = jnp.sum(onehot * csum, axis=1) - 1
    gsz = ((counts + tm - 1) // tm) * tm
    ends = jnp.cumsum(gsz)
    slot = (ends - gsz)[e] + rank
    n_slots = A + N_EXP * tm
    asg = jnp.full((n_slots,), -1, I32).at[slot].set(jnp.arange(A, dtype=I32))
    n_tiles = n_slots // tm
    tile_exp = jnp.searchsorted(ends, jnp.arange(n_tiles, dtype=I32) * tm, side="right").astype(I32)
    n_used = (ends[-1] // tm).astype(I32)
    return asg, jnp.minimum(tile_exp, N_EXP - 1), n_used.reshape(1)


def _expert_kernel(te_ref, asg_ref, nu_ref, h_hbm, wg_ref, bg_ref, wu_ref, bu_ref, wd_ref, bd_ref,
                   y_hbm, xbuf, ybuf, sem_in, sem_out, *, tm, n_tok):
    i = pl.program_id(0)

    def gather_copy(r):
        tok = jnp.maximum(asg_ref[i * tm + r], 0) >> 2
        return pltpu.make_async_copy(h_hbm.at[pl.ds(tok, 1)], xbuf.at[pl.ds(r, 1)], sem_in)

    def scatter_copy(r, a):
        row = (a & (TOP_K - 1)) * n_tok + (a >> 2)
        return pltpu.make_async_copy(ybuf.at[pl.ds(r, 1)], y_hbm.at[pl.ds(row, 1)], sem_out)

    @pl.when(i < nu_ref[0])
    def _():
        def start_in(r, c):
            gather_copy(r).start()
            return c
        lax.fori_loop(0, tm, start_in, 0)

        def wait_in(r, c):
            gather_copy(r).wait()
            return c
        lax.fori_loop(0, tm, wait_in, 0)

        x = xbuf[...].astype(BF16)
        gate = jnp.minimum(_dot(x, wg_ref[0]) + bg_ref[0], SWIGLU_LIMIT)
        up = jnp.clip(_dot(x, wu_ref[0]) + bu_ref[0], -SWIGLU_LIMIT, SWIGLU_LIMIT)
        act = (up + 1.0) * gate * _sigmoid(SWIGLU_ALPHA * gate)
        ybuf[...] = _dot(act.astype(BF16), wd_ref[0]) + bd_ref[0]

        def start_out(r, c):
            a = asg_ref[i * tm + r]

            @pl.when(a >= 0)
            def _():
                scatter_copy(r, a).start()
            return c
        lax.fori_loop(0, tm, start_out, 0)

        def wait_out(r, c):
            a = asg_ref[i * tm + r]

            @pl.when(a >= 0)
            def _():
                scatter_copy(r, a).wait()
            return c
        lax.fori_loop(0, tm, wait_out, 0)


def _experts(h2d, asg, tile_exp, n_used, wg, bg, wu, bu, wd, bd, *, tm):
    T = h2d.shape[0]
    n_tiles = tile_exp.shape[0]
    wspec = lambda: pl.BlockSpec((1, D, D), lambda i, te, a, n: (te[i], 0, 0))
    bspec = lambda: pl.BlockSpec((1, 1, D), lambda i, te, a, n: (te[i], 0, 0))
    return pl.pallas_call(
        functools.partial(_expert_kernel, tm=tm, n_tok=T),
        grid_spec=pltpu.PrefetchScalarGridSpec(
            num_scalar_prefetch=3,
            grid=(n_tiles,),
            in_specs=[pl.BlockSpec(memory_space=pl.ANY), wspec(), bspec(), wspec(), bspec(), wspec(), bspec()],
            out_specs=pl.BlockSpec(memory_space=pl.ANY),
            scratch_shapes=[pltpu.VMEM((tm, D), F32), pltpu.VMEM((tm, D), F32),
                            pltpu.SemaphoreType.DMA(()), pltpu.SemaphoreType.DMA(())]),
        out_shape=jax.ShapeDtypeStruct((TOP_K * T, D), F32),
        compiler_params=_cparams(("arbitrary",)),
        name="moe_experts",
    )(tile_exp, asg, n_used, h2d, wg, bg, wu, bu, wd, bd)


def _combine_kernel(y0_ref, y1_ref, y2_ref, y3_ref, tw_ref, x_ref, g2_ref, lg_ref, lb_ref, o_ref, *, alpha):
    tw = tw_ref[...]
    f = (tw[:, 0:1] * y0_ref[...] + tw[:, 1:2] * y1_ref[...] + tw[:, 2:3] * y2_ref[...] + tw[:, 3:4] * y3_ref[...])
    o_ref[...] = _layer_norm(alpha * x_ref[...] + g2_ref[0] * f, lg_ref[...], lb_ref[...])


def _combine(ybuf, tw2d, x2d, g2, lg, lb, *, B, NL, NC, alpha):
    T = x2d.shape[0]
    TR = NC
    nt = (NL + NC) // TR
    n_lat_tiles = NL // TR
    nblk = T // TR
    yspec = lambda kk: pl.BlockSpec((TR, D), lambda n: (kk * nblk + n, 0))
    mod_idx = lambda n: (jnp.where(n % nt == n_lat_tiles, B, n // nt), 0, 0)
    return pl.pallas_call(
        functools.partial(_combine_kernel, alpha=alpha),
        grid=(nblk,),
        in_specs=[yspec(0), yspec(1), yspec(2), yspec(3), pl.BlockSpec((TR, 128), lambda n: (n, 0)),
                  pl.BlockSpec((TR, D), lambda n: (n, 0)), pl.BlockSpec((1, 1, D), mod_idx),
                  pl.BlockSpec((1, D), lambda n: (0, 0)), pl.BlockSpec((1, D), lambda n: (0, 0))],
        out_specs=pl.BlockSpec((TR, D), lambda n: (n, 0)),
        out_shape=jax.ShapeDtypeStruct((T, D), F32),
        compiler_params=_cparams(("arbitrary",)),
        name="moe_combine_ln",
    )(ybuf, ybuf, ybuf, ybuf, tw2d, x2d, g2, lg, lb)


def _gdn_scan(dq, dk, dv, dz, dg, nw, ones_bd, *, NL, NC):
    B, TT, _ = dq.shape
    blk = lambda c=256: pl.BlockSpec((1, TT, c), lambda b: (b, 0, 0))
    return pl.pallas_call(
        functools.partial(_gdn_kernel, NL=NL, NC=NC),
        grid=(B,),
        in_specs=[blk(), blk(), blk(), blk(), blk(128), pl.BlockSpec((1, 256), lambda b: (0, 0)),
                  pl.BlockSpec((256, 256), lambda b: (0, 0))],
        out_specs=blk(),
        out_shape=jax.ShapeDtypeStruct((B, TT, 256), F32),
        scratch_shapes=[pltpu.VMEM((TT, 256), F32), pltpu.VMEM((256, 256), F32)],
        compiler_params=_cparams(("arbitrary",)),
        name="gdn_scan",
    )(dq, dk, dv, dz, dg, nw, ones_bd)


def _expert_tile_rows(n_assign):
    for tm in (512, 256, 128):
        if n_assign % tm == 0:
            return tm
    raise ValueError("token count must be a multiple of 32")


def kernel(x, c, ctx, c_ctx, w_ada, b_ada, w_in, hy_conv_w, hy_conv_b, hy_w1, hy_b1, hy_w2, hy_b2, hy_w3, hy_b3, hy_w4, hy_freq, hy_d, sc_conv_w, gla_w_a, gla_b_a, gla_norm_w, gdn_conv_w, gdn_a_log, gdn_dt_bias, gdn_norm_w, w_out, ln1_g, ln1_b, w_router, b_router, w_gate, b_gate, w_up, b_up, w_down, b_down, ln2_g, ln2_b):
    B, NL, _ = x.shape
    NC = ctx.shape[1]
    TT = NL + NC
    T = B * TT
    depth = w_ada.shape[0]
    alpha = (2 * depth) ** 0.25
    assert NL % NC == 0 and NC % CHUNK == 0 and (NC & (NC - 1)) == 0

    xcat = jnp.concatenate([x, ctx], axis=1)
    n_mod = -(-(B + 1) // 8) * 8
    cc = jnp.concatenate([c, c_ctx[None], jnp.zeros((n_mod - B - 1, D), F32)], axis=0)
    mods = _ada_mods(cc, w_ada, b_ada)

    pp = _prep_inproj_params(w_in, hy_conv_w, hy_conv_b, hy_d, sc_conv_w, gla_w_a, gla_b_a, gdn_conv_w,
                             gdn_a_log, gdn_dt_bias)
    ones_bd = _block_ones(256, 64)
    hy_args = (jnp.pad(hy_w1, ((0, 0), (0, HY_HID - HY_EMB), (0, 0))), hy_b1[:, None], hy_w2, hy_b2[:, None],
               hy_w3, hy_b3[:, None], hy_w4, hy_freq)
    tabs_l = _dft_tables(NL)
    tabs_c = _dft_tables(NC)
    spec_l = _hyena_spectrum(NL, tabs_l[0], tabs_l[1], *hy_args)
    spec_c = _hyena_spectrum(NC, tabs_c[0], tabs_c[1], *hy_args)
    tabs_l = tuple(t.astype(BF16) for t in tabs_l)
    tabs_c = tuple(t.astype(BF16) for t in tabs_c)

    w_out_b = w_out.astype(BF16)
    wr = _pad_cols(w_router, 128)
    br = jnp.pad(b_router, ((0, 0), (0, 128 - N_EXP)), constant_values=NEG_BIG)[:, None, :]
    wg, wu, wd = w_gate.astype(BF16), w_up.astype(BF16), w_down.astype(BF16)
    tm = _expert_tile_rows(T * TOP_K)

    for l in range(depth):
        sh1, sc1, g1, sh2, sc2, g2 = (mods[l, :, k * D:(k + 1) * D][:, None, :] for k in range(6))
        zb, e, x0, ysc, gqk, gv, gr, gg, dq, dk, dv, dz, dg = _inproj(
            xcat, 1.0 + sc1, sh1, pp["w"][l], pp["hyw"][l], pp["hyb"][l], pp["hyd"][l], pp["scw"][l],
            pp["glaw"][l], pp["glab"][l], pp["gdnw"][l], pp["gdna"][l], pp["gdnd"][l], ones_bd, NL=NL, NC=NC)
        ya_lat = _hyconv(zb, e, x0, spec_l[l:l + 1], tabs_l, L=NL, row_blk=0)
        ya_ctx = _hyconv(zb, e, x0, spec_c[l:l + 1], tabs_c, L=NC, row_blk=NL // NC)
        yc = _gla_scan(gqk, gv, gr, gg, jnp.tile(gla_norm_w[l], 4)[None], ones_bd, NL=NL, NC=NC)
        yd = _gdn_scan(dq, dk, dv, dz, dg, jnp.tile(gdn_norm_w[l], 4)[None], ones_bd, NL=NL, NC=NC)
        x1, h, ti, tw = _outproj(ya_lat, ya_ctx, ysc, yc, yd, xcat, g1, 1.0 + sc2, sh2, w_out_b[l], ln1_g[l][None],
                                 ln1_b[l][None], wr[l], br[l], NL=NL, NC=NC, alpha=alpha)
        asg, tile_exp, n_used = _route(ti.reshape(T, 128)[:, :TOP_K], tm)
        ybuf = _experts(h.reshape(T, D), asg, tile_exp, n_used, wg[l], b_gate[l][:, None, :], wu[l],
                        b_up[l][:, None, :], wd[l], b_down[l][:, None, :], tm=tm)
        x2 = _combine(ybuf, tw.reshape(T, 128), x1.reshape(T, D), g2, ln2_g[l][None], ln2_b[l][None],
                      B=B, NL=NL, NC=NC, alpha=alpha)
        xcat = x2.reshape(B, TT, D)
    return xcat[:, :NL]
```

```python
import functools
import math

import numpy as np
import jax
import jax.numpy as jnp
from jax import lax
from jax.experimental import pallas as pl
from jax.experimental.pallas import tpu as pltpu

F32 = jnp.float32
BF16 = jnp.bfloat16
I32 = jnp.int32
HI = lax.Precision.HIGHEST

D = 1024
GW = 256
CHUNK = 64
ROW_W = 64
GLA_DK = 32
GLA_TAU = 16.0
GDN_DK = 64
N_EXP = 32
TOP_K = 4
SWIGLU_LIMIT = 7.0
SWIGLU_ALPHA = 1.702
LN_EPS = 1e-5
RMS_EPS = 1e-6
L2_EPS = 1e-6
HY_EMB = 33
HY_HID = 64
NEG_BIG = -1e30
P_HY, P_SC, P_GLA, P_GDN = 768, 768, 896, 1152
P_IN = P_HY + P_SC + P_GLA + P_GDN
VMEM_LIMIT = 56 * 1024 * 1024


def _cparams(sem):
    return pltpu.CompilerParams(dimension_semantics=sem, vmem_limit_bytes=VMEM_LIMIT)


def _split_bf16(x):
    hi = x.astype(BF16)
    lo = (x - hi.astype(F32)).astype(BF16)
    return hi, lo


def _dot(a, b):
    return jnp.dot(a, b, preferred_element_type=F32)


def _dot_nt(a, b):
    return lax.dot_general(a, b, (((1,), (1,)), ((), ())), preferred_element_type=F32)


def _dot_tn(a, b):
    return lax.dot_general(a, b, (((0,), (0,)), ((), ())), preferred_element_type=F32)


def _dot_sel(x, sel_bf16):
    h1 = x.astype(BF16)
    r1 = x - h1.astype(F32)
    h2 = r1.astype(BF16)
    h3 = (r1 - h2.astype(F32)).astype(BF16)
    return _dot(h1, sel_bf16) + _dot(h2, sel_bf16) + _dot(h3, sel_bf16)


def _sel_dot(sel_bf16, x):
    h1 = x.astype(BF16)
    r1 = x - h1.astype(F32)
    h2 = r1.astype(BF16)
    h3 = (r1 - h2.astype(F32)).astype(BF16)
    return _dot(sel_bf16, h1) + _dot(sel_bf16, h2) + _dot(sel_bf16, h3)


def _iota(shape, dim):
    return lax.broadcasted_iota(I32, shape, dim)


def _sigmoid(x):
    return 1.0 / (1.0 + jnp.exp(-x))


def _silu(x):
    return x * _sigmoid(x)


def _softplus(x):
    return jnp.maximum(x, 0.0) + jnp.log(1.0 + jnp.exp(-jnp.abs(x)))


def _log_sigmoid(x):
    return -_softplus(-x)


def _ada_kernel(cc_ref, w_ref, b_ref, o_ref):
    o_ref[0] = jnp.dot(_silu(cc_ref[...]), w_ref[0], precision=HI, preferred_element_type=F32) + b_ref[0]


def _ada_mods(cc, w_ada, b_ada):
    depth, _, n6 = w_ada.shape
    rows = cc.shape[0]
    tn = 1536
    return pl.pallas_call(
        _ada_kernel,
        grid=(depth, n6 // tn),
        in_specs=[pl.BlockSpec((rows, D), lambda l, j: (0, 0)),
                  pl.BlockSpec((1, D, tn), lambda l, j: (l, 0, j)),
                  pl.BlockSpec((1, 1, tn), lambda l, j: (l, 0, j))],
        out_specs=pl.BlockSpec((1, rows, tn), lambda l, j: (l, 0, j)),
        out_shape=jax.ShapeDtypeStruct((depth, rows, n6), F32),
        compiler_params=_cparams(("arbitrary", "arbitrary")),
        name="ada_mods",
    )(cc, w_ada, b_ada.reshape(depth, 1, n6))


def _conv3(x, w, rowlen):
    n = x.shape[0]
    pos = _iota(x.shape, 0) & (rowlen - 1)
    xm = jnp.where(pos == 0, 0.0, pltpu.roll(x, 1, axis=0))
    xp = jnp.where(pos == rowlen - 1, 0.0, pltpu.roll(x, n - 1, axis=0))
    return xm * w[0:1] + x * w[1:2] + xp * w[2:3]


def _inproj_kernel(x_ref, scp_ref, sh_ref, w_ref, hyw_ref, hyb_ref, hyd_ref, scw_ref, glaw_ref, glab_ref,
                   gdnw_ref, gdna_ref, gdnd_ref, ones_ref,
                   zb_ref, e_ref, x0_ref, ysc_ref, gqk_ref, gv_ref, gr_ref, gg_ref,
                   dq_ref, dk_ref, dv_ref, dz_ref, dg_ref, *, n_lat_tiles, ctx_len):
    i = pl.program_id(1)
    rowlen = jnp.where(i == n_lat_tiles, ctx_len, ROW_W)
    xm = (x_ref[0] * scp_ref[0] + sh_ref[0]).astype(BF16)

    u = _conv3(_dot(xm, w_ref[:, 0:P_HY]), hyw_ref[...], rowlen) + hyb_ref[...]
    z = u[:, GW:2 * GW] * u[:, 2 * GW:3 * GW]
    zb_ref[0] = z.astype(BF16)
    e_ref[0] = z * hyd_ref[...]
    x0_ref[0] = u[:, 0:GW]

    ps = _dot(xm, w_ref[:, P_HY:P_HY + P_SC])
    ysc_ref[0] = ps[:, 0:GW] * _conv3(ps[:, GW:2 * GW] * ps[:, 2 * GW:3 * GW], scw_ref[...], rowlen)

    pg = _dot(xm, w_ref[:, P_HY + P_SC:P_HY + P_SC + P_GLA])
    gqk_ref[0] = pg[:, 0:256]
    gv_ref[0] = pg[:, 256:512]
    gr_ref[0] = pg[:, 512:768]
    ah, al = _split_bf16(pg[:, 768:896])
    wh, wl = _split_bf16(glaw_ref[...])
    logit = _dot(ah, wh) + _dot(al, wh) + _dot(ah, wl) + glab_ref[...]
    gg_ref[0] = _log_sigmoid(logit) * (1.0 / GLA_TAU)

    o3 = P_HY + P_SC + P_GLA
    pd = _dot(xm, w_ref[:, o3:o3 + P_GDN])
    qkv = _silu(_conv3(pd[:, 0:768], gdnw_ref[...], rowlen))
    ones_bd = ones_ref[...]
    q = qkv[:, 0:256]
    k = qkv[:, 256:512]
    dq_ref[0] = q * lax.rsqrt(_dot_sel(q * q, ones_bd) + L2_EPS) * (GDN_DK ** -0.5)
    dk_ref[0] = k * lax.rsqrt(_dot_sel(k * k, ones_bd) + L2_EPS)
    dv_ref[0] = qkv[:, 512:768]
    dz_ref[0] = pd[:, 768:1024]
    gl = pd[:, 1024:1152]
    lane = _iota(gl.shape, 1)
    dg_ref[0] = jnp.where(lane < 8, gdna_ref[...] * _softplus(gl + gdnd_ref[...]), _sigmoid(gl))


def _inproj(xcat, scp, sh, w_bf16, hyw, hyb, hyd, scw, glaw, glab, gdnw, gdna, gdnd, ones_bd, *, NL, NC):
    B, TT, _ = xcat.shape
    TR = NC
    n_lat_tiles = NL // TR
    nt = TT // TR
    mod_idx = lambda b, i: (jnp.where(i == n_lat_tiles, B, b), 0, 0)
    const2 = lambda b, i: (0, 0)
    row_blk = lambda c: pl.BlockSpec((1, TR, c), lambda b, i: (b, i, 0))
    outs = [(GW, BF16)] + [(GW, F32)] * 11 + [(128, F32)]
    return pl.pallas_call(
        functools.partial(_inproj_kernel, n_lat_tiles=n_lat_tiles, ctx_len=NC),
        grid=(B, nt),
        in_specs=[row_blk(D),
                  pl.BlockSpec((1, 1, D), mod_idx), pl.BlockSpec((1, 1, D), mod_idx),
                  pl.BlockSpec((D, P_IN), const2),
                  pl.BlockSpec((3, P_HY), const2), pl.BlockSpec((1, P_HY), const2), pl.BlockSpec((1, GW), const2),
                  pl.BlockSpec((3, GW), const2),
                  pl.BlockSpec((128, 256), const2), pl.BlockSpec((1, 256), const2),
                  pl.BlockSpec((3, 768), const2), pl.BlockSpec((1, 128), const2), pl.BlockSpec((1, 128), const2),
                  pl.BlockSpec((256, 256), const2)],
        out_specs=[row_blk(c) for c, _ in outs],
        out_shape=[jax.ShapeDtypeStruct((B, TT, c), dt) for c, dt in outs],
        compiler_params=_cparams(("arbitrary", "arbitrary")),
        name="inproj_local",
    )(xcat, scp, sh, w_bf16, hyw, hyb, hyd, scw, glaw, glab, gdnw, gdna, gdnd, ones_bd)


def _dft_tables(L):
    N = 2 * L
    f = jnp.arange(L, dtype=I32)[:, None]
    t = jnp.arange(L, dtype=I32)[None, :]
    ang = ((f * t) % N).astype(F32) * (2.0 * math.pi / N)
    cos = jnp.cos(ang)
    sin = jnp.sin(ang)
    nyq = jnp.where(t % 2 == 0, 1.0, -1.0).astype(F32)
    mc = cos
    ms = jnp.where(f == 0, nyq, -sin)
    wf = jnp.where(f == 0, 1.0, 2.0).astype(F32) / N
    ic = (cos * wf).T
    isn = jnp.where(f == 0, nyq / N, -sin * wf).T
    return mc, ms, ic, isn


def _hyfilt_kernel(zf_ref, win_ref, w1_ref, b1_ref, w2_ref, b2_ref, w3_ref, b3_ref, w4_ref, fr_ref,
                   mc_ref, ms_ref, o_ref, ks_ref):
    j = pl.program_id(1)

    @pl.when(j == 0)
    def _():
        dot = lambda a, b: jnp.dot(a, b, precision=HI, preferred_element_type=F32)
        h = jnp.sin(fr_ref[0, 0:1] * (dot(zf_ref[...], w1_ref[0]) + b1_ref[0]))
        h = jnp.sin(fr_ref[0, 1:2] * (dot(h, w2_ref[0]) + b2_ref[0]))
        h = jnp.sin(fr_ref[0, 2:3] * (dot(h, w3_ref[0]) + b3_ref[0]))
        k = dot(h, w4_ref[0])
        win = win_ref[...]
        kf = k[:, 0:GW] * win
        kb = k[:, GW:2 * GW] * win
        kb = jnp.where(_iota(kb.shape, 0) == 0, 0.0, kb)
        l1 = jnp.sum(jnp.abs(kf), axis=0, keepdims=True) + jnp.sum(jnp.abs(kb), axis=0, keepdims=True)
        kf = kf / l1
        kb = kb / l1
        ks_ref[:, 0:GW] = kf + kb
        ks_ref[:, GW:2 * GW] = kf - kb

    ks = ks_ref[...]
    o_ref[0, 0] = jnp.dot(mc_ref[...], ks, precision=HI, preferred_element_type=F32)
    o_ref[0, 1] = jnp.dot(ms_ref[...], ks, precision=HI, preferred_element_type=F32)


def _hyena_spectrum(L, mc, ms, w1p, b1, w2, b2, w3, b3, w4, freq):
    depth = w1p.shape[0]
    t = jnp.linspace(0.0, 1.0, L, dtype=F32)[:, None]
    bands = (HY_EMB - 1) // 2
    ang = 2.0 * math.pi * jnp.arange(L, dtype=F32)[:, None] / L
    f = jnp.linspace(1e-4, bands - 1, bands, dtype=F32)[None, :]
    zf = jnp.concatenate([t, jnp.cos(f * ang), -jnp.sin(f * ang), jnp.zeros((L, HY_HID - HY_EMB), F32)], axis=-1)
    max_decay = math.log(1e-2) / 0.3
    min_decay = math.log(1e-2) / 1.5
    deltas = jnp.abs(jnp.linspace(min_decay, max_decay, GW, dtype=F32))
    win = jnp.exp(-t * deltas[None, :])
    ft = min(L, 512)
    c2 = lambda l, j: (0, 0)
    lw = lambda *s: pl.BlockSpec((1,) + s, lambda l, j: (l,) + (0,) * len(s))
    a = pl.pallas_call(
        _hyfilt_kernel,
        grid=(depth, L // ft),
        in_specs=[pl.BlockSpec((L, HY_HID), c2), pl.BlockSpec((L, GW), c2),
                  lw(HY_HID, HY_HID), lw(1, HY_HID), lw(HY_HID, HY_HID), lw(1, HY_HID),
                  lw(HY_HID, HY_HID), lw(1, HY_HID), lw(HY_HID, 2 * GW), lw(3, HY_HID),
                  pl.BlockSpec((ft, L), lambda l, j: (j, 0)), pl.BlockSpec((ft, L), lambda l, j: (j, 0))],
        out_specs=pl.BlockSpec((1, 2, ft, 2 * GW), lambda l, j: (l, 0, j, 0)),
        out_shape=jax.ShapeDtypeStruct((depth, 2, L, 2 * GW), F32),
        scratch_shapes=[pltpu.VMEM((L, 2 * GW), F32)],
        compiler_params=_cparams(("arbitrary", "arbitrary")),
        name="hyena_filter_spectrum",
    )(zf, win, w1p, b1, w2, b2, w3, b3, w4, freq, mc, ms)
    k_re = a[:, 0, :, 0:GW]
    k_im = a[:, 1, :, GW:2 * GW]
    k_im = k_im.at[:, 0, :].set(a[:, 1, 0, 0:GW])
    return jnp.stack([k_re, k_im], axis=1)


def _hyconv_kernel(zb_ref, e_ref, x0_ref, k_ref, mc_ref, ms_ref, ic_ref, is_ref, o_ref, acc_ref, *, G):
    j = pl.program_id(1)
    nj = pl.num_programs(1)
    kt = k_ref[0, 0]
    kb = k_ref[0, 1]
    row0 = jnp.logical_and(_iota(kt.shape, 0) == 0, j == 0)
    mc = mc_ref[...]
    ms = ms_ref[...]
    ic = ic_ref[...]
    isn = is_ref[...]
    for g in range(G):
        z = zb_ref[g]
        zt = _dot(mc, z)
        zi = _dot(ms, z)
        yt = zt * kt - jnp.where(row0, 0.0, zi * kb)
        yi = jnp.where(row0, zi * kb, zt * kb + zi * kt)
        contrib = _dot(ic, yt.astype(BF16)) + _dot(isn, yi.astype(BF16))

        @pl.when(j == 0)
        def _():
            acc_ref[g] = contrib

        @pl.when(j > 0)
        def _():
            acc_ref[g] += contrib

    @pl.when(j == nj - 1)
    def _():
        for g in range(G):
            o_ref[g] = (acc_ref[g] + e_ref[g]) * x0_ref[g]


def _hyconv(zb, e, x0, kspec_l, tabs, *, L, row_blk):
    B = zb.shape[0]
    G = 2 if B % 2 == 0 else 1
    ft = min(L, 256)
    mc, ms, ic, isn = tabs
    seg = lambda: pl.BlockSpec((G, L, GW), lambda b, j: (b, row_blk, 0))
    return pl.pallas_call(
        functools.partial(_hyconv_kernel, G=G),
        grid=(B // G, L // ft),
        in_specs=[seg(), seg(), seg(),
                  pl.BlockSpec((1, 2, ft, GW), lambda b, j: (0, 0, j, 0)),
                  pl.BlockSpec((ft, L), lambda b, j: (j, 0)), pl.BlockSpec((ft, L), lambda b, j: (j, 0)),
                  pl.BlockSpec((L, ft), lambda b, j: (0, j)), pl.BlockSpec((L, ft), lambda b, j: (0, j))],
        out_specs=pl.BlockSpec((G, L, GW), lambda b, j: (b, 0, 0)),
        out_shape=jax.ShapeDtypeStruct((B, L, GW), F32),
        scratch_shapes=[pltpu.VMEM((G, L, GW), F32)],
        compiler_params=_cparams(("arbitrary", "arbitrary")),
        name=f"hyena_longconv_{L}",
    )(zb, e, x0, kspec_l, mc, ms, ic, isn)


def _pad_cols(a, n):
    return jnp.pad(a, [(0, 0)] * (a.ndim - 1) + [(0, n - a.shape[-1])])


def _prep_inproj_params(w_in, hy_conv_w, hy_conv_b, hy_d, sc_conv_w, gla_w_a, gla_b_a, gdn_conv_w, gdn_a_log,
                        gdn_dt_bias):
    depth = w_in.shape[0]
    o1, o2, o3 = 768, 1536, 1536 + 800
    w = jnp.concatenate([w_in[..., 0:o2], _pad_cols(w_in[..., o2:o3], P_GLA), _pad_cols(w_in[..., o3:], P_GDN)],
                        axis=-1).astype(BF16)
    glaw = jnp.zeros((depth, 128, 256), F32)
    glaw = glaw.at[:, 0:16, 0:128].set(gla_w_a[:, 0]).at[:, 16:32, 128:256].set(gla_w_a[:, 1])
    glab = gla_b_a.reshape(depth, 1, 256)
    gdna = _pad_cols(-jnp.exp(gdn_a_log.astype(F32)).reshape(depth, 1, 8), 128)
    gdnd = _pad_cols(gdn_dt_bias.astype(F32).reshape(depth, 1, 8), 128)
    return dict(w=w, hyw=hy_conv_w, hyb=hy_conv_b[:, None, :], hyd=hy_d[:, None, :], scw=sc_conv_w, glaw=glaw,
                glab=glab, gdnw=gdn_conv_w, gdna=gdna, gdnd=gdnd)


def _block_ones(n, blk):
    r = np.arange(n)
    return jnp.asarray((r[:, None] // blk) == (r[None, :] // blk), BF16)


def _tri_pair():
    r = _iota((CHUNK, CHUNK), 0)
    c = _iota((CHUNK, CHUNK), 1)
    return [(r >= c).astype(BF16), (r <= c).astype(BF16)]


def _cat_masks():
    r = _iota((CHUNK, 4 * CHUNK), 0)
    c = _iota((CHUNK, 4 * CHUNK), 1) & (CHUNK - 1)
    return r, c


SCAN_G = 2


def _scan_loops(NL, NC, G, chunk_fn, finish_fn):
    def phase(n, base0):
        def body(i, carry):
            bf = pl.multiple_of(base0 + i * CHUNK, CHUNK)
            bb = pl.multiple_of(base0 + (n - 1 - i) * CHUNK, CHUNK)
            chunk_fn([(g, d, (bf, bb)[d]) for g in range(G) for d in range(2)])
            return carry
        lax.fori_loop(0, n, body, 0)

    phase(NC // CHUNK, NL)
    phase(NL // CHUNK, 0)

    def fin(i, carry):
        base = pl.multiple_of(i * NC, NC)
        for g in range(G):
            finish_fn(g, pl.ds(base, NC))
        return carry
    lax.fori_loop(0, (NL + NC) // NC, fin, 0)


def _gla_kernel(qk_ref, v_ref, r_ref, g_ref, nw_ref, ones_ref, o_ref, od_ref, st_ref, *, NL, NC, G):
    C = CHUNK
    scale = GLA_DK ** -0.5
    tri = _tri_pair()
    ri, ci = _cat_masks()
    cmask = [ci <= ri, ci >= ri]
    hm = (_iota((256, 128), 0) >> 6) == (_iota((256, 128), 1) >> 5)
    vbd = (_iota((256, 256), 0) >> 6) == (_iota((256, 256), 1) >> 6)

    def chunk(chains):
        n = range(len(chains))
        G_ = [c[0] for c in chains]
        D_ = [c[1] for c in chains]
        R_ = [pl.ds(c[2], C) for c in chains]
        mid = [C // 2 if d == 0 else C // 2 - 1 for d in D_]
        last = [C - 1 if d == 0 else 0 for d in D_]
        q = [qk_ref[G_[j], R_[j], 0:128] * scale for j in n]
        k = [qk_ref[G_[j], R_[j], 128:256] for j in n]
        v = [v_ref[G_[j], R_[j], :].astype(BF16) for j in n]
        g = [g_ref[G_[j], R_[j], D_[j] * 128:(D_[j] + 1) * 128] for j in n]
        b = [_sel_dot(tri[D_[j]], g[j]) for j in n]
        b_mid = [b[j][mid[j]:mid[j] + 1] for j in n]
        b_last = [b[j][last[j]:last[j] + 1] for j in n]
        qe = [(q[j] * jnp.exp(b[j] - b_mid[j])).astype(BF16) for j in n]
        ke = [(k[j] * jnp.exp(b_mid[j] - b[j])).astype(BF16) for j in n]
        kbd = [jnp.where(hm, jnp.concatenate([ke[j]] * 4, axis=0), jnp.zeros((), BF16)) for j in n]
        a = [jnp.where(cmask[D_[j]], _dot_nt(qe[j], kbd[j]), 0.0).astype(BF16) for j in n]
        vb = [jnp.where(vbd, jnp.concatenate([v[j]] * 4, axis=0), jnp.zeros((), BF16)) for j in n]
        st = [st_ref[G_[j], D_[j]] for j in n]
        qb = [(q[j] * jnp.exp(b[j])).astype(BF16) for j in n]
        o = [_dot(a[j], vb[j]) + _dot_nt(qb[j], st[j].astype(BF16)) for j in n]
        kd = [(k[j] * jnp.exp(b_last[j] - b[j])).astype(BF16) for j in n]
        upd = [_dot_tn(v[j], kd[j]) for j in n]
        for j in n:
            od_ref[G_[j], D_[j], R_[j], :] = o[j]
            st_ref[G_[j], D_[j]] = st[j] * jnp.exp(b_last[j]) + jnp.where(hm, upd[j], 0.0)

    def finish(gi, rows):
        o = od_ref[gi, 0, rows, :] + od_ref[gi, 1, rows, :]
        ms = _dot_sel(o * o, ones_ref[...]) * (1.0 / 64.0)
        o_ref[gi, rows, :] = o * lax.rsqrt(ms + RMS_EPS) * nw_ref[...] * _silu(r_ref[gi, rows, :])

    st_ref[...] = jnp.zeros(st_ref.shape, F32)
    _scan_loops(NL, NC, G, chunk, finish)


def _scan_call(kernel_fn, name, arrays, nw, ones_bd, state_cols, *, NL, NC):
    B, TT, _ = arrays[0].shape
    G = SCAN_G if B % SCAN_G == 0 else 1
    blk = lambda c: pl.BlockSpec((G, TT, c), lambda b: (b, 0, 0), pipeline_mode=pl.Buffered(1))
    return pl.pallas_call(
        functools.partial(kernel_fn, NL=NL, NC=NC, G=G),
        grid=(B // G,),
        in_specs=[blk(a.shape[-1]) for a in arrays] + [pl.BlockSpec((1, 256), lambda b: (0, 0)),
                                                        pl.BlockSpec((256, 256), lambda b: (0, 0))],
        out_specs=pl.BlockSpec((G, TT, 256), lambda b: (b, 0, 0)),
        out_shape=jax.ShapeDtypeStruct((B, TT, 256), F32),
        scratch_shapes=[pltpu.VMEM((G, 2, TT, 256), F32), pltpu.VMEM((G, 2, 256, state_cols), F32)],
        compiler_params=_cparams(("arbitrary",)),
        name=name,
    )(*arrays, nw, ones_bd)


def _gla_scan(gqk, gv, gr, gg, nw, ones_bd, *, NL, NC):
    return _scan_call(_gla_kernel, "gla_scan", (gqk, gv, gr, gg), nw, ones_bd, 128, NL=NL, NC=NC)


def _gdn_kernel(q_ref, k_ref, v_ref, z_ref, g_ref, nw_ref, ones_ref, o_ref, od_ref, s_ref, *, NL, NC, G):
    C = CHUNK
    tri = _tri_pair()
    ri, ci = _cat_masks()
    m_incl = [ci <= ri, ci >= ri]
    m_strict = [ci < ri, ci > ri]
    dmask = ci == ri
    eye_cat = dmask.astype(F32)
    bd = (_iota((256, 256), 0) >> 6) == (_iota((256, 256), 1) >> 6)
    ones64 = jnp.ones((C, C), BF16)
    lane_r = _iota((128, 256), 0)
    head_c = _iota((128, 256), 1) >> 6
    eg = [(lane_r == head_c + 4 * d).astype(BF16) for d in range(2)]
    eb = [(lane_r == head_c + 8 + 4 * d).astype(BF16) for d in range(2)]

    def block_diag(y):
        return jnp.where(bd, jnp.concatenate([y.astype(BF16)] * 4, axis=0), jnp.zeros((), BF16))

    def mm_cat(x, y):
        return _dot(x.astype(BF16), block_diag(y))

    def pair_mask(ls):
        return jnp.logical_and((ri >> (ls + 1)) == (ci >> (ls + 1)), (ri >> ls) != (ci >> ls))

    def chunk(chains):
        n = range(len(chains))
        G_ = [c[0] for c in chains]
        D_ = [c[1] for c in chains]
        R_ = [pl.ds(c[2], C) for c in chains]
        last = [C - 1 if d == 0 else 0 for d in D_]
        q = [q_ref[G_[j], R_[j], :] for j in n]
        k = [k_ref[G_[j], R_[j], :] for j in n]
        v = [v_ref[G_[j], R_[j], :] for j in n]
        gt = [g_ref[G_[j], R_[j], :] for j in n]
        cum = [_sel_dot(tri[D_[j]], gt[j]) for j in n]
        cexp = [_dot_sel(cum[j], eg[D_[j]]) for j in n]
        ct = [_sel_dot(ones64, jnp.where(dmask, cexp[j], 0.0)) for j in n]
        decay = [jnp.exp(jnp.where(m_incl[D_[j]], cexp[j] - ct[j], -jnp.inf)) for j in n]
        bexp = [_dot_sel(gt[j], eb[D_[j]]) for j in n]
        kb = [k[j] * bexp[j] for j in n]
        kbd = [block_diag(k[j]) for j in n]
        a = [jnp.where(m_strict[D_[j]], _dot_nt(kb[j].astype(BF16), kbd[j]) * decay[j], 0.0) for j in n]
        t = [eye_cat - jnp.where(pair_mask(0), a[j], 0.0) for j in n]
        for ls in range(1, 6):
            ta = [mm_cat(t[j], jnp.where(pair_mask(ls), a[j], 0.0)) for j in n]
            t = [t[j] - mm_cat(ta[j], t[j]) for j in n]
        ecum = [jnp.exp(cexp[j]) for j in n]
        u = [mm_cat(t[j], v[j] * bexp[j]) for j in n]
        w = [mm_cat(t[j], kb[j] * ecum[j]) for j in n]
        s = [s_ref[G_[j], D_[j]] for j in n]
        sb = [s[j].astype(BF16) for j in n]
        v_new = [u[j] - _dot(w[j].astype(BF16), sb[j]) for j in n]
        attn = [jnp.where(m_incl[D_[j]], _dot_nt(q[j].astype(BF16), kbd[j]) * decay[j], 0.0) for j in n]
        o = [_dot((q[j] * ecum[j]).astype(BF16), sb[j]) + mm_cat(attn[j], v_new[j]) for j in n]
        cl = [cexp[j][last[j]:last[j] + 1] for j in n]
        kd = [(k[j] * jnp.exp(cl[j] - cexp[j])).astype(BF16) for j in n]
        upd = [_dot_tn(kd[j], v_new[j].astype(BF16)) for j in n]
        for j in n:
            od_ref[G_[j], D_[j], R_[j], :] = o[j]
            s_ref[G_[j], D_[j]] = s[j] * jnp.exp(cl[j]) + jnp.where(bd, upd[j], 0.0)

    def finish(gi, rows):
        o = od_ref[gi, 0, rows, :] + od_ref[gi, 1, rows, :]
        ms = _dot_sel(o * o, ones_ref[...]) * (1.0 / 64.0)
        o_ref[gi, rows, :] = o * lax.rsqrt(ms + RMS_EPS) * nw_ref[...] * _silu(z_ref[gi, rows, :])

    s_ref[...] = jnp.zeros(s_ref.shape, F32)
    _scan_loops(NL, NC, G, chunk, finish)


def _layer_norm(t, g, b):
    tc = t - jnp.mean(t, axis=-1, keepdims=True)
    var = jnp.mean(tc * tc, axis=-1, keepdims=True)
    return tc * lax.rsqrt(var + LN_EPS) * g + b


def _outproj_kernel(yal_ref, yac_ref, yb_ref, yc_ref, yd_ref, x_ref, g1_ref, scp_ref, sh_ref, w_ref, lg_ref, lb_ref,
                    wr_ref, br_ref, x1_ref, h_ref, ti_ref, tw_ref, *, alpha, n_lat_tiles):
    ya = jnp.where(pl.program_id(1) == n_lat_tiles, yac_ref[0], yal_ref[0])
    mix = (_dot(ya.astype(BF16), w_ref[0:GW]) + _dot(yb_ref[0].astype(BF16), w_ref[GW:2 * GW])
           + _dot(yc_ref[0].astype(BF16), w_ref[2 * GW:3 * GW]) + _dot(yd_ref[0].astype(BF16), w_ref[3 * GW:4 * GW]))
    x1 = _layer_norm(alpha * x_ref[0] + g1_ref[0] * mix, lg_ref[...], lb_ref[...])
    x1_ref[0] = x1
    h = x1 * scp_ref[0] + sh_ref[0]
    for s in range(8):
        h_ref[0, :, s, :] = h[:, s * 128:(s + 1) * 128]
    logits = jnp.dot(h, wr_ref[...], precision=HI, preferred_element_type=F32) + br_ref[...]
    lane = _iota(logits.shape, 1)
    vals, idxs = [], []
    for _ in range(TOP_K):
        m = jnp.max(logits, axis=-1, keepdims=True)
        idx = jnp.min(jnp.where(logits == m, lane, 128), axis=-1, keepdims=True)
        vals.append(m)
        idxs.append(idx)
        logits = jnp.where(lane == idx, -jnp.inf, logits)
    ex = [jnp.exp(v - vals[0]) for v in vals]
    inv = 1.0 / (ex[0] + ex[1] + ex[2] + ex[3])
    ti = jnp.zeros(lane.shape, I32)
    tw = jnp.zeros(lane.shape, F32)
    for kk in range(TOP_K):
        ti = jnp.where(lane == kk, idxs[kk], ti)
        tw = jnp.where(lane == kk, ex[kk] * inv, tw)
    ti_ref[0] = ti
    tw_ref[0] = tw


def _outproj(ya_lat, ya_ctx, yb, yc, yd, xcat, g1, scp, sh, w_bf16, lg, lb, wr, br, *, NL, NC, alpha):
    B, TT, _ = xcat.shape
    TR = NC
    n_lat_tiles = NL // TR
    mod_idx = lambda b, i: (jnp.where(i == n_lat_tiles, B, b), 0, 0)
    const2 = lambda b, i: (0, 0)
    row_blk = lambda c: pl.BlockSpec((1, TR, c), lambda b, i: (b, i, 0))
    mod = lambda: pl.BlockSpec((1, 1, D), mod_idx)
    return pl.pallas_call(
        functools.partial(_outproj_kernel, alpha=alpha, n_lat_tiles=n_lat_tiles),
        grid=(B, TT // TR),
        in_specs=[pl.BlockSpec((1, TR, GW), lambda b, i: (b, jnp.minimum(i, n_lat_tiles - 1), 0)),
                  pl.BlockSpec((1, TR, GW), lambda b, i: (b, 0, 0)),
                  row_blk(GW), row_blk(GW), row_blk(GW), row_blk(D), mod(), mod(), mod(),
                  pl.BlockSpec((D, D), const2), pl.BlockSpec((1, D), const2), pl.BlockSpec((1, D), const2),
                  pl.BlockSpec((D, 128), const2), pl.BlockSpec((1, 128), const2)],
        out_specs=[row_blk(D), pl.BlockSpec((1, TR, 8, 128), lambda b, i: (b, i, 0, 0)), row_blk(128), row_blk(128)],
        out_shape=[jax.ShapeDtypeStruct((B, TT, D), F32), jax.ShapeDtypeStruct((B, TT, 8, 128), F32),
                   jax.ShapeDtypeStruct((B, TT, 128), I32), jax.ShapeDtypeStruct((B, TT, 128), F32)],
        compiler_params=_cparams(("arbitrary", "arbitrary")),
        name="outproj_ln_router",
    )(ya_lat, ya_ctx, yb, yc, yd, xcat, g1, scp, sh, w_bf16, lg, lb, wr, br)


def _route(tidx, tm):
    T = tidx.shape[0]
    A = T * TOP_K
    e = tidx.reshape(A)
    onehot = (e[:, None] == jnp.arange(N_EXP, dtype=I32)[None, :]).astype(I32)
    csum = jnp.cumsum(onehot, axis=0)
    counts = csum[-1]
    rank = jnp.sum(onehot * csum, axis=1) - 1
    gsz = ((counts + tm - 1) // tm) * tm
    ends = jnp.cumsum(gsz)
    slot = (ends - gsz)[e] + rank
    n_slots = A + N_EXP * tm
    asg = jnp.full((n_slots,), -1, I32).at[slot].set(jnp.arange(A, dtype=I32))
    n_tiles = n_slots // tm
    tile_start = jnp.arange(n_tiles, dtype=I32) * tm
    tile_exp = jnp.sum((ends[None, :] <= tile_start[:, None]).astype(I32), axis=1)
    n_used = (ends[-1] // tm).astype(I32)
    return asg, jnp.minimum(tile_exp, N_EXP - 1), n_used.reshape(1)


WAIT_UNROLL = 16


def _expert_kernel(te_ref, asg_ref, nu_ref, h_hbm, wg_ref, bg_ref, wu_ref, bu_ref, wd_ref, bd_ref,
                   y_hbm, xbuf, ybuf, sem_in, sem_out, *, tm, n_tok):
    i = pl.program_id(0)
    nu = nu_ref[0]
    cur = i & 1

    def start_gather(tile, sl):
        def body(r, c):
            tok = jnp.maximum(asg_ref[tile * tm + r], 0) >> 2
            pltpu.make_async_copy(h_hbm.at[tok], xbuf.at[sl, r], sem_in.at[sl]).start()
            return c
        lax.fori_loop(0, tm, body, 0, unroll=8)

    def start_scatter(tile, sl):
        def body(r, c):
            a = asg_ref[tile * tm + r]
            row = jnp.where(a >= 0, (a & (TOP_K - 1)) * n_tok + (a >> 2), TOP_K * n_tok + sl * tm + r)
            pltpu.make_async_copy(ybuf.at[sl, r], y_hbm.at[row], sem_out.at[sl]).start()
            return c
        lax.fori_loop(0, tm, body, 0, unroll=8)

    def wait_rows(src_row, dst_row, sem):
        def body(j, c):
            for _ in range(WAIT_UNROLL):
                pltpu.make_async_copy(src_row, dst_row, sem).wait()
            return c
        lax.fori_loop(0, tm // WAIT_UNROLL, body, 0)

    def wait_gather(sl):
        wait_rows(h_hbm.at[0], xbuf.at[sl, 0], sem_in.at[sl])

    def wait_scatter(sl):
        wait_rows(ybuf.at[sl, 0], y_hbm.at[0], sem_out.at[sl])

    def compute(sl):
        x = jnp.concatenate([xbuf[sl, :, s, :] for s in range(8)], axis=1).astype(BF16)
        gate = jnp.minimum(_dot(x, wg_ref[0, 0]) + bg_ref[0, 0], SWIGLU_LIMIT)
        up = jnp.clip(_dot(x, wu_ref[0, 0]) + bu_ref[0, 0], -SWIGLU_LIMIT, SWIGLU_LIMIT)
        act = (up + 1.0) * gate * _sigmoid(SWIGLU_ALPHA * gate)
        y = _dot(act.astype(BF16), wd_ref[0, 0]) + bd_ref[0, 0]
        for s in range(8):
            ybuf[sl, :, s, :] = y[:, s * 128:(s + 1) * 128]

    @pl.when(i < nu)
    def _():
        @pl.when(i == 0)
        def _():
            start_gather(0, 0)

        @pl.when(i + 1 < nu)
        def _():
            start_gather(i + 1, 1 - cur)

        wait_gather(cur)

        @pl.when(i >= 2)
        def _():
            wait_scatter(cur)

        for sl in range(2):
            @pl.when(cur == sl)
            def _():
                compute(sl)

        start_scatter(i, cur)

        @pl.when(i == nu - 1)
        def _():
            wait_scatter(cur)

            @pl.when(i >= 1)
            def _():
                wait_scatter(1 - cur)


def _experts(h3d, asg, tile_exp, n_used, wg, bg, wu, bu, wd, bd, *, tm, layer):
    T = h3d.shape[0]
    n_tiles = tile_exp.shape[0]
    wspec = lambda: pl.BlockSpec((1, 1, D, D), lambda i, te, a, n: (layer, te[i], 0, 0))
    bspec = lambda: pl.BlockSpec((1, 1, 1, D), lambda i, te, a, n: (layer, te[i], 0, 0))
    return pl.pallas_call(
        functools.partial(_expert_kernel, tm=tm, n_tok=T),
        grid_spec=pltpu.PrefetchScalarGridSpec(
            num_scalar_prefetch=3,
            grid=(n_tiles,),
            in_specs=[pl.BlockSpec(memory_space=pl.ANY), wspec(), bspec(), wspec(), bspec(), wspec(), bspec()],
            out_specs=pl.BlockSpec(memory_space=pl.ANY),
            scratch_shapes=[pltpu.VMEM((2, tm, 8, 128), F32), pltpu.VMEM((2, tm, 8, 128), F32),
                            pltpu.SemaphoreType.DMA((2,)), pltpu.SemaphoreType.DMA((2,))]),
        out_shape=jax.ShapeDtypeStruct((TOP_K * T + 2 * tm, 8, 128), F32),
        compiler_params=_cparams(("arbitrary",)),
        name="moe_experts",
    )(tile_exp, asg, n_used, h3d, wg, bg, wu, bu, wd, bd)


def _combine_kernel(y0_ref, y1_ref, y2_ref, y3_ref, tw_ref, x_ref, g2_ref, lg_ref, lb_ref, o_ref, *, alpha):
    tw = tw_ref[...]
    rows = lambda y_ref: jnp.concatenate([y_ref[:, s, :] for s in range(8)], axis=1)
    f = (tw[:, 0:1] * rows(y0_ref) + tw[:, 1:2] * rows(y1_ref) + tw[:, 2:3] * rows(y2_ref)
         + tw[:, 3:4] * rows(y3_ref))
    o_ref[...] = _layer_norm(alpha * x_ref[...] + g2_ref[0] * f, lg_ref[...], lb_ref[...])


def _combine(ybuf, tw2d, x2d, g2, lg, lb, *, B, NL, NC, alpha):
    T = x2d.shape[0]
    TR = NC
    nt = (NL + NC) // TR
    n_lat_tiles = NL // TR
    nblk = T // TR
    yspec = lambda kk: pl.BlockSpec((TR, 8, 128), lambda n: (kk * nblk + n, 0, 0))
    mod_idx = lambda n: (jnp.where(n % nt == n_lat_tiles, B, n // nt), 0, 0)
    return pl.pallas_call(
        functools.partial(_combine_kernel, alpha=alpha),
        grid=(nblk,),
        in_specs=[yspec(0), yspec(1), yspec(2), yspec(3), pl.BlockSpec((TR, 128), lambda n: (n, 0)),
                  pl.BlockSpec((TR, D), lambda n: (n, 0)), pl.BlockSpec((1, 1, D), mod_idx),
                  pl.BlockSpec((1, D), lambda n: (0, 0)), pl.BlockSpec((1, D), lambda n: (0, 0))],
        out_specs=pl.BlockSpec((TR, D), lambda n: (n, 0)),
        out_shape=jax.ShapeDtypeStruct((T, D), F32),
        compiler_params=_cparams(("arbitrary",)),
        name="moe_combine_ln",
    )(ybuf, ybuf, ybuf, ybuf, tw2d, x2d, g2, lg, lb)


def _gdn_scan(dq, dk, dv, dz, dg, nw, ones_bd, *, NL, NC):
    return _scan_call(_gdn_kernel, "gdn_scan", (dq, dk, dv, dz, dg), nw, ones_bd, 256, NL=NL, NC=NC)


def _expert_tile_rows(n_assign):
    for tm in (512, 256, 128):
        if n_assign % tm == 0:
            return tm
    raise ValueError("token count must be a multiple of 32")


def kernel(x, c, ctx, c_ctx, w_ada, b_ada, w_in, hy_conv_w, hy_conv_b, hy_w1, hy_b1, hy_w2, hy_b2, hy_w3, hy_b3, hy_w4, hy_freq, hy_d, sc_conv_w, gla_w_a, gla_b_a, gla_norm_w, gdn_conv_w, gdn_a_log, gdn_dt_bias, gdn_norm_w, w_out, ln1_g, ln1_b, w_router, b_router, w_gate, b_gate, w_up, b_up, w_down, b_down, ln2_g, ln2_b):
    B, NL, _ = x.shape
    NC = ctx.shape[1]
    TT = NL + NC
    T = B * TT
    depth = w_ada.shape[0]
    alpha = (2 * depth) ** 0.25
    assert NL % NC == 0 and NC % CHUNK == 0 and (NC & (NC - 1)) == 0

    xcat = jnp.concatenate([x, ctx], axis=1)
    n_mod = -(-(B + 1) // 8) * 8
    cc = jnp.concatenate([c, c_ctx[None], jnp.zeros((n_mod - B - 1, D), F32)], axis=0)
    mods = _ada_mods(cc, w_ada, b_ada)

    pp = _prep_inproj_params(w_in, hy_conv_w, hy_conv_b, hy_d, sc_conv_w, gla_w_a, gla_b_a, gdn_conv_w,
                             gdn_a_log, gdn_dt_bias)
    ones_bd = _block_ones(256, 64)
    hy_args = (jnp.pad(hy_w1, ((0, 0), (0, HY_HID - HY_EMB), (0, 0))), hy_b1[:, None], hy_w2, hy_b2[:, None],
               hy_w3, hy_b3[:, None], hy_w4, hy_freq)
    tabs_l = _dft_tables(NL)
    tabs_c = _dft_tables(NC)
    spec_l = _hyena_spectrum(NL, tabs_l[0], tabs_l[1], *hy_args)
    spec_c = _hyena_spectrum(NC, tabs_c[0], tabs_c[1], *hy_args)
    tabs_l = tuple(t.astype(BF16) for t in tabs_l)
    tabs_c = tuple(t.astype(BF16) for t in tabs_c)

    w_out_b = w_out.astype(BF16)
    wr = _pad_cols(w_router, 128)
    br = jnp.pad(b_router, ((0, 0), (0, 128 - N_EXP)), constant_values=NEG_BIG)[:, None, :]
    wg, wu, wd = w_gate.astype(BF16), w_up.astype(BF16), w_down.astype(BF16)
    tm = _expert_tile_rows(T * TOP_K)

    for l in range(depth):
        sh1, sc1, g1, sh2, sc2, g2 = (mods[l, :, k * D:(k + 1) * D][:, None, :] for k in range(6))
        zb, e, x0, ysc, gqk, gv, gr, gg, dq, dk, dv, dz, dg = _inproj(
            xcat, 1.0 + sc1, sh1, pp["w"][l], pp["hyw"][l], pp["hyb"][l], pp["hyd"][l], pp["scw"][l],
            pp["glaw"][l], pp["glab"][l], pp["gdnw"][l], pp["gdna"][l], pp["gdnd"][l], ones_bd, NL=NL, NC=NC)
        ya_lat = _hyconv(zb, e, x0, spec_l[l:l + 1], tabs_l, L=NL, row_blk=0)
        ya_ctx = _hyconv(zb, e, x0, spec_c[l:l + 1], tabs_c, L=NC, row_blk=NL // NC)
        yc = _gla_scan(gqk, gv, gr, gg, jnp.tile(gla_norm_w[l], 4)[None], ones_bd, NL=NL, NC=NC)
        yd = _gdn_scan(dq, dk, dv, dz, dg, jnp.tile(gdn_norm_w[l], 4)[None], ones_bd, NL=NL, NC=NC)
        x1, h, ti, tw = _outproj(ya_lat, ya_ctx, ysc, yc, yd, xcat, g1, 1.0 + sc2, sh2, w_out_b[l], ln1_g[l][None],
                                 ln1_b[l][None], wr[l], br[l], NL=NL, NC=NC, alpha=alpha)
        asg, tile_exp, n_used = _route(ti.reshape(T, 128)[:, :TOP_K], tm)
        ybuf = _experts(h.reshape(T, 8, 128), asg, tile_exp, n_used, wg, b_gate[:, :, None, :], wu,
                        b_up[:, :, None, :], wd, b_down[:, :, None, :], tm=tm, layer=l)
        x2 = _combine(ybuf, tw.reshape(T, 128), x1.reshape(T, D), g2, ln2_g[l][None], ln2_b[l][None],
                      B=B, NL=NL, NC=NC, alpha=alpha)
        xcat = x2.reshape(B, TT, D)
    return xcat[:, :NL]
```

```python
import functools
import math

import numpy as np
import jax
import jax.numpy as jnp
from jax import lax
from jax.experimental import pallas as pl
from jax.experimental.pallas import tpu as pltpu

F32 = jnp.float32
BF16 = jnp.bfloat16
I32 = jnp.int32
HI = lax.Precision.HIGHEST

D = 1024
GW = 256
CHUNK = 64
ROW_W = 64
GLA_DK = 32
GLA_TAU = 16.0
GDN_DK = 64
N_EXP = 32
TOP_K = 4
SWIGLU_LIMIT = 7.0
SWIGLU_ALPHA = 1.702
LN_EPS = 1e-5
RMS_EPS = 1e-6
L2_EPS = 1e-6
HY_EMB = 33
HY_HID = 64
NEG_BIG = -1e30
P_HY, P_SC, P_GLA, P_GDN = 768, 768, 896, 1152
P_IN = P_HY + P_SC + P_GLA + P_GDN
VMEM_LIMIT = 56 * 1024 * 1024


def _cparams(sem):
    return pltpu.CompilerParams(dimension_semantics=sem, vmem_limit_bytes=VMEM_LIMIT)


def _split_bf16(x):
    hi = x.astype(BF16)
    lo = (x - hi.astype(F32)).astype(BF16)
    return hi, lo


def _dot(a, b):
    return jnp.dot(a, b, preferred_element_type=F32)


def _dot_nt(a, b):
    return lax.dot_general(a, b, (((1,), (1,)), ((), ())), preferred_element_type=F32)


def _dot_tn(a, b):
    return lax.dot_general(a, b, (((0,), (0,)), ((), ())), preferred_element_type=F32)


def _dot_sel(x, sel_bf16):
    h1 = x.astype(BF16)
    r1 = x - h1.astype(F32)
    h2 = r1.astype(BF16)
    h3 = (r1 - h2.astype(F32)).astype(BF16)
    return _dot(h1, sel_bf16) + _dot(h2, sel_bf16) + _dot(h3, sel_bf16)


def _sel_dot(sel_bf16, x):
    h1 = x.astype(BF16)
    r1 = x - h1.astype(F32)
    h2 = r1.astype(BF16)
    h3 = (r1 - h2.astype(F32)).astype(BF16)
    return _dot(sel_bf16, h1) + _dot(sel_bf16, h2) + _dot(sel_bf16, h3)


def _iota(shape, dim):
    return lax.broadcasted_iota(I32, shape, dim)


def _sigmoid(x):
    return 1.0 / (1.0 + jnp.exp(-x))


def _silu(x):
    return x * _sigmoid(x)


def _softplus(x):
    return jnp.maximum(x, 0.0) + jnp.log(1.0 + jnp.exp(-jnp.abs(x)))


def _log_sigmoid(x):
    return -_softplus(-x)


def _ada_kernel(cc_ref, w_ref, b_ref, o_ref):
    o_ref[0] = jnp.dot(_silu(cc_ref[...]), w_ref[0], precision=HI, preferred_element_type=F32) + b_ref[0]


def _ada_mods(cc, w_ada, b_ada):
    depth, _, n6 = w_ada.shape
    rows = cc.shape[0]
    tn = 1536
    return pl.pallas_call(
        _ada_kernel,
        grid=(depth, n6 // tn),
        in_specs=[pl.BlockSpec((rows, D), lambda l, j: (0, 0)),
                  pl.BlockSpec((1, D, tn), lambda l, j: (l, 0, j)),
                  pl.BlockSpec((1, 1, tn), lambda l, j: (l, 0, j))],
        out_specs=pl.BlockSpec((1, rows, tn), lambda l, j: (l, 0, j)),
        out_shape=jax.ShapeDtypeStruct((depth, rows, n6), F32),
        compiler_params=_cparams(("arbitrary", "arbitrary")),
        name="ada_mods",
    )(cc, w_ada, b_ada.reshape(depth, 1, n6))


def _conv3(x, w, rowlen):
    n = x.shape[0]
    pos = _iota(x.shape, 0) & (rowlen - 1)
    xm = jnp.where(pos == 0, 0.0, pltpu.roll(x, 1, axis=0))
    xp = jnp.where(pos == rowlen - 1, 0.0, pltpu.roll(x, n - 1, axis=0))
    return xm * w[0:1] + x * w[1:2] + xp * w[2:3]


def _inproj_kernel(x_ref, scp_ref, sh_ref, w_ref, hyw_ref, hyb_ref, hyd_ref, scw_ref, glaw_ref, glab_ref,
                   gdnw_ref, gdna_ref, gdnd_ref, ones_ref,
                   zb_ref, e_ref, x0_ref, ysc_ref, gqk_ref, gv_ref, gr_ref, gg_ref,
                   dq_ref, dk_ref, dv_ref, dz_ref, dg_ref, *, n_lat_tiles, ctx_len):
    i = pl.program_id(1)
    rowlen = jnp.where(i == n_lat_tiles, ctx_len, ROW_W)
    xm = (x_ref[0] * scp_ref[0] + sh_ref[0]).astype(BF16)

    u = _conv3(_dot(xm, w_ref[:, 0:P_HY]), hyw_ref[...], rowlen) + hyb_ref[...]
    z = u[:, GW:2 * GW] * u[:, 2 * GW:3 * GW]
    zb_ref[0] = z.astype(BF16)
    e_ref[0] = z * hyd_ref[...]
    x0_ref[0] = u[:, 0:GW]

    ps = _dot(xm, w_ref[:, P_HY:P_HY + P_SC])
    ysc_ref[0] = ps[:, 0:GW] * _conv3(ps[:, GW:2 * GW] * ps[:, 2 * GW:3 * GW], scw_ref[...], rowlen)

    pg = _dot(xm, w_ref[:, P_HY + P_SC:P_HY + P_SC + P_GLA])
    gqk_ref[0] = pg[:, 0:256]
    gv_ref[0] = pg[:, 256:512]
    gr_ref[0] = pg[:, 512:768]
    ah, al = _split_bf16(pg[:, 768:896])
    wh, wl = _split_bf16(glaw_ref[...])
    logit = _dot(ah, wh) + _dot(al, wh) + _dot(ah, wl) + glab_ref[...]
    gg_ref[0] = _log_sigmoid(logit) * (1.0 / GLA_TAU)

    o3 = P_HY + P_SC + P_GLA
    pd = _dot(xm, w_ref[:, o3:o3 + P_GDN])
    qkv = _silu(_conv3(pd[:, 0:768], gdnw_ref[...], rowlen))
    ones_bd = ones_ref[...]
    q = qkv[:, 0:256]
    k = qkv[:, 256:512]
    dq_ref[0] = q * lax.rsqrt(_dot_sel(q * q, ones_bd) + L2_EPS) * (GDN_DK ** -0.5)
    dk_ref[0] = k * lax.rsqrt(_dot_sel(k * k, ones_bd) + L2_EPS)
    dv_ref[0] = qkv[:, 512:768]
    dz_ref[0] = pd[:, 768:1024]
    gl = pd[:, 1024:1152]
    lane = _iota(gl.shape, 1)
    dg_ref[0] = jnp.where(lane < 8, gdna_ref[...] * _softplus(gl + gdnd_ref[...]), _sigmoid(gl))


def _inproj(xcat, scp, sh, w_bf16, hyw, hyb, hyd, scw, glaw, glab, gdnw, gdna, gdnd, ones_bd, *, NL, NC):
    B, TT, _ = xcat.shape
    TR = NC
    n_lat_tiles = NL // TR
    nt = TT // TR
    mod_idx = lambda b, i: (jnp.where(i == n_lat_tiles, B, b), 0, 0)
    const2 = lambda b, i: (0, 0)
    row_blk = lambda c: pl.BlockSpec((1, TR, c), lambda b, i: (b, i, 0))
    outs = [(GW, BF16)] + [(GW, F32)] * 11 + [(128, F32)]
    return pl.pallas_call(
        functools.partial(_inproj_kernel, n_lat_tiles=n_lat_tiles, ctx_len=NC),
        grid=(B, nt),
        in_specs=[row_blk(D),
                  pl.BlockSpec((1, 1, D), mod_idx), pl.BlockSpec((1, 1, D), mod_idx),
                  pl.BlockSpec((D, P_IN), const2),
                  pl.BlockSpec((3, P_HY), const2), pl.BlockSpec((1, P_HY), const2), pl.BlockSpec((1, GW), const2),
                  pl.BlockSpec((3, GW), const2),
                  pl.BlockSpec((128, 256), const2), pl.BlockSpec((1, 256), const2),
                  pl.BlockSpec((3, 768), const2), pl.BlockSpec((1, 128), const2), pl.BlockSpec((1, 128), const2),
                  pl.BlockSpec((256, 256), const2)],
        out_specs=[row_blk(c) for c, _ in outs],
        out_shape=[jax.ShapeDtypeStruct((B, TT, c), dt) for c, dt in outs],
        compiler_params=_cparams(("arbitrary", "arbitrary")),
        name="inproj_local",
    )(xcat, scp, sh, w_bf16, hyw, hyb, hyd, scw, glaw, glab, gdnw, gdna, gdnd, ones_bd)


def _dft_tables(L):
    N = 2 * L
    f = jnp.arange(L, dtype=I32)[:, None]
    t = jnp.arange(L, dtype=I32)[None, :]
    ang = ((f * t) % N).astype(F32) * (2.0 * math.pi / N)
    cos = jnp.cos(ang)
    sin = jnp.sin(ang)
    nyq = jnp.where(t % 2 == 0, 1.0, -1.0).astype(F32)
    mc = cos
    ms = jnp.where(f == 0, nyq, -sin)
    wf = jnp.where(f == 0, 1.0, 2.0).astype(F32) / N
    ic = (cos * wf).T
    isn = jnp.where(f == 0, nyq / N, -sin * wf).T
    return mc, ms, ic, isn


def _hyfilt_kernel(zf_ref, win_ref, w1_ref, b1_ref, w2_ref, b2_ref, w3_ref, b3_ref, w4_ref, fr_ref,
                   mc_ref, ms_ref, o_ref, ks_ref):
    j = pl.program_id(1)

    @pl.when(j == 0)
    def _():
        dot = lambda a, b: jnp.dot(a, b, precision=HI, preferred_element_type=F32)
        h = jnp.sin(fr_ref[0, 0:1] * (dot(zf_ref[...], w1_ref[0]) + b1_ref[0]))
        h = jnp.sin(fr_ref[0, 1:2] * (dot(h, w2_ref[0]) + b2_ref[0]))
        h = jnp.sin(fr_ref[0, 2:3] * (dot(h, w3_ref[0]) + b3_ref[0]))
        k = dot(h, w4_ref[0])
        win = win_ref[...]
        kf = k[:, 0:GW] * win
        kb = k[:, GW:2 * GW] * win
        kb = jnp.where(_iota(kb.shape, 0) == 0, 0.0, kb)
        l1 = jnp.sum(jnp.abs(kf), axis=0, keepdims=True) + jnp.sum(jnp.abs(kb), axis=0, keepdims=True)
        kf = kf / l1
        kb = kb / l1
        ks_ref[:, 0:GW] = kf + kb
        ks_ref[:, GW:2 * GW] = kf - kb

    ks = ks_ref[...]
    o_ref[0, 0] = jnp.dot(mc_ref[...], ks, precision=HI, preferred_element_type=F32)
    o_ref[0, 1] = jnp.dot(ms_ref[...], ks, precision=HI, preferred_element_type=F32)


def _hyena_spectrum(L, mc, ms, w1p, b1, w2, b2, w3, b3, w4, freq):
    depth = w1p.shape[0]
    t = jnp.linspace(0.0, 1.0, L, dtype=F32)[:, None]
    bands = (HY_EMB - 1) // 2
    ang = 2.0 * math.pi * jnp.arange(L, dtype=F32)[:, None] / L
    f = jnp.linspace(1e-4, bands - 1, bands, dtype=F32)[None, :]
    zf = jnp.concatenate([t, jnp.cos(f * ang), -jnp.sin(f * ang), jnp.zeros((L, HY_HID - HY_EMB), F32)], axis=-1)
    max_decay = math.log(1e-2) / 0.3
    min_decay = math.log(1e-2) / 1.5
    deltas = jnp.abs(jnp.linspace(min_decay, max_decay, GW, dtype=F32))
    win = jnp.exp(-t * deltas[None, :])
    ft = min(L, 512)
    c2 = lambda l, j: (0, 0)
    lw = lambda *s: pl.BlockSpec((1,) + s, lambda l, j: (l,) + (0,) * len(s))
    a = pl.pallas_call(
        _hyfilt_kernel,
        grid=(depth, L // ft),
        in_specs=[pl.BlockSpec((L, HY_HID), c2), pl.BlockSpec((L, GW), c2),
                  lw(HY_HID, HY_HID), lw(1, HY_HID), lw(HY_HID, HY_HID), lw(1, HY_HID),
                  lw(HY_HID, HY_HID), lw(1, HY_HID), lw(HY_HID, 2 * GW), lw(3, HY_HID),
                  pl.BlockSpec((ft, L), lambda l, j: (j, 0)), pl.BlockSpec((ft, L), lambda l, j: (j, 0))],
        out_specs=pl.BlockSpec((1, 2, ft, 2 * GW), lambda l, j: (l, 0, j, 0)),
        out_shape=jax.ShapeDtypeStruct((depth, 2, L, 2 * GW), F32),
        scratch_shapes=[pltpu.VMEM((L, 2 * GW), F32)],
        compiler_params=_cparams(("arbitrary", "arbitrary")),
        name="hyena_filter_spectrum",
    )(zf, win, w1p, b1, w2, b2, w3, b3, w4, freq, mc, ms)
    k_re = a[:, 0, :, 0:GW]
    k_im = a[:, 1, :, GW:2 * GW]
    k_im = k_im.at[:, 0, :].set(a[:, 1, 0, 0:GW])
    return jnp.stack([k_re, k_im], axis=1)


def _hyconv_kernel(zb_ref, e_ref, x0_ref, k_ref, mc_ref, ms_ref, ic_ref, is_ref, o_ref, acc_ref, *, G):
    j = pl.program_id(1)
    nj = pl.num_programs(1)
    kt = k_ref[0, 0]
    kb = k_ref[0, 1]
    row0 = jnp.logical_and(_iota(kt.shape, 0) == 0, j == 0)
    mc = mc_ref[...]
    ms = ms_ref[...]
    ic = ic_ref[...]
    isn = is_ref[...]
    for g in range(G):
        z = zb_ref[g]
        zt = _dot(mc, z)
        zi = _dot(ms, z)
        yt = zt * kt - jnp.where(row0, 0.0, zi * kb)
        yi = jnp.where(row0, zi * kb, zt * kb + zi * kt)
        contrib = _dot(ic, yt.astype(BF16)) + _dot(isn, yi.astype(BF16))

        @pl.when(j == 0)
        def _():
            acc_ref[g] = contrib

        @pl.when(j > 0)
        def _():
            acc_ref[g] += contrib

    @pl.when(j == nj - 1)
    def _():
        for g in range(G):
            o_ref[g] = (acc_ref[g] + e_ref[g]) * x0_ref[g]


def _hyconv(zb, e, x0, kspec_l, tabs, *, L, row_blk):
    B = zb.shape[0]
    G = 2 if B % 2 == 0 else 1
    ft = min(L, 256)
    mc, ms, ic, isn = tabs
    seg = lambda: pl.BlockSpec((G, L, GW), lambda b, j: (b, row_blk, 0))
    return pl.pallas_call(
        functools.partial(_hyconv_kernel, G=G),
        grid=(B // G, L // ft),
        in_specs=[seg(), seg(), seg(),
                  pl.BlockSpec((1, 2, ft, GW), lambda b, j: (0, 0, j, 0)),
                  pl.BlockSpec((ft, L), lambda b, j: (j, 0)), pl.BlockSpec((ft, L), lambda b, j: (j, 0)),
                  pl.BlockSpec((L, ft), lambda b, j: (0, j)), pl.BlockSpec((L, ft), lambda b, j: (0, j))],
        out_specs=pl.BlockSpec((G, L, GW), lambda b, j: (b, 0, 0)),
        out_shape=jax.ShapeDtypeStruct((B, L, GW), F32),
        scratch_shapes=[pltpu.VMEM((G, L, GW), F32)],
        compiler_params=_cparams(("arbitrary", "arbitrary")),
        name=f"hyena_longconv_{L}",
    )(zb, e, x0, kspec_l, mc, ms, ic, isn)


def _pad_cols(a, n):
    return jnp.pad(a, [(0, 0)] * (a.ndim - 1) + [(0, n - a.shape[-1])])


def _prep_inproj_params(w_in, hy_conv_w, hy_conv_b, hy_d, sc_conv_w, gla_w_a, gla_b_a, gdn_conv_w, gdn_a_log,
                        gdn_dt_bias):
    depth = w_in.shape[0]
    o1, o2, o3 = 768, 1536, 1536 + 800
    w = jnp.concatenate([w_in[..., 0:o2], _pad_cols(w_in[..., o2:o3], P_GLA), _pad_cols(w_in[..., o3:], P_GDN)],
                        axis=-1).astype(BF16)
    glaw = jnp.zeros((depth, 128, 256), F32)
    glaw = glaw.at[:, 0:16, 0:128].set(gla_w_a[:, 0]).at[:, 16:32, 128:256].set(gla_w_a[:, 1])
    glab = gla_b_a.reshape(depth, 1, 256)
    gdna = _pad_cols(-jnp.exp(gdn_a_log.astype(F32)).reshape(depth, 1, 8), 128)
    gdnd = _pad_cols(gdn_dt_bias.astype(F32).reshape(depth, 1, 8), 128)
    return dict(w=w, hyw=hy_conv_w, hyb=hy_conv_b[:, None, :], hyd=hy_d[:, None, :], scw=sc_conv_w, glaw=glaw,
                glab=glab, gdnw=gdn_conv_w, gdna=gdna, gdnd=gdnd)


def _block_ones(n, blk):
    r = np.arange(n)
    return jnp.asarray((r[:, None] // blk) == (r[None, :] // blk), BF16)


def _tri_pair():
    r = _iota((CHUNK, CHUNK), 0)
    c = _iota((CHUNK, CHUNK), 1)
    return [(r >= c).astype(BF16), (r <= c).astype(BF16)]


def _cat_masks():
    r = _iota((CHUNK, 4 * CHUNK), 0)
    c = _iota((CHUNK, 4 * CHUNK), 1) & (CHUNK - 1)
    return r, c


SCAN_G = 2


def _scan_loops(NL, NC, G, chunk_fn, finish_fn):
    def phase(n, base0):
        def body(i, carry):
            bf = pl.multiple_of(base0 + i * CHUNK, CHUNK)
            bb = pl.multiple_of(base0 + (n - 1 - i) * CHUNK, CHUNK)
            chunk_fn([(g, d, (bf, bb)[d]) for g in range(G) for d in range(2)])
            return carry
        lax.fori_loop(0, n, body, 0)

    phase(NC // CHUNK, NL)
    phase(NL // CHUNK, 0)

    def fin(i, carry):
        base = pl.multiple_of(i * NC, NC)
        for g in range(G):
            finish_fn(g, pl.ds(base, NC))
        return carry
    lax.fori_loop(0, (NL + NC) // NC, fin, 0)


def _gla_kernel(qk_ref, v_ref, r_ref, g_ref, nw_ref, ones_ref, o_ref, od_ref, st_ref, *, NL, NC, G):
    C = CHUNK
    scale = GLA_DK ** -0.5
    tri = _tri_pair()
    ri, ci = _cat_masks()
    cmask = [ci <= ri, ci >= ri]
    hm = (_iota((256, 128), 0) >> 6) == (_iota((256, 128), 1) >> 5)
    vbd = (_iota((256, 256), 0) >> 6) == (_iota((256, 256), 1) >> 6)

    def chunk(chains):
        n = range(len(chains))
        G_ = [c[0] for c in chains]
        D_ = [c[1] for c in chains]
        R_ = [pl.ds(c[2], C) for c in chains]
        mid = [C // 2 if d == 0 else C // 2 - 1 for d in D_]
        last = [C - 1 if d == 0 else 0 for d in D_]
        q = [qk_ref[G_[j], R_[j], 0:128] * scale for j in n]
        k = [qk_ref[G_[j], R_[j], 128:256] for j in n]
        v = [v_ref[G_[j], R_[j], :].astype(BF16) for j in n]
        g = [g_ref[G_[j], R_[j], D_[j] * 128:(D_[j] + 1) * 128] for j in n]
        b = [_sel_dot(tri[D_[j]], g[j]) for j in n]
        b_mid = [b[j][mid[j]:mid[j] + 1] for j in n]
        b_last = [b[j][last[j]:last[j] + 1] for j in n]
        qe = [(q[j] * jnp.exp(b[j] - b_mid[j])).astype(BF16) for j in n]
        ke = [(k[j] * jnp.exp(b_mid[j] - b[j])).astype(BF16) for j in n]
        kbd = [jnp.where(hm, jnp.concatenate([ke[j]] * 4, axis=0), jnp.zeros((), BF16)) for j in n]
        a = [jnp.where(cmask[D_[j]], _dot_nt(qe[j], kbd[j]), 0.0).astype(BF16) for j in n]
        vb = [jnp.where(vbd, jnp.concatenate([v[j]] * 4, axis=0), jnp.zeros((), BF16)) for j in n]
        st = [st_ref[G_[j], D_[j]] for j in n]
        qb = [(q[j] * jnp.exp(b[j])).astype(BF16) for j in n]
        o = [_dot(a[j], vb[j]) + _dot_nt(qb[j], st[j].astype(BF16)) for j in n]
        kd = [(k[j] * jnp.exp(b_last[j] - b[j])).astype(BF16) for j in n]
        upd = [_dot_tn(v[j], kd[j]) for j in n]
        for j in n:
            od_ref[G_[j], D_[j], R_[j], :] = o[j]
            st_ref[G_[j], D_[j]] = st[j] * jnp.exp(b_last[j]) + jnp.where(hm, upd[j], 0.0)

    def finish(gi, rows):
        o = od_ref[gi, 0, rows, :] + od_ref[gi, 1, rows, :]
        ms = _dot_sel(o * o, ones_ref[...]) * (1.0 / 64.0)
        o_ref[gi, rows, :] = o * lax.rsqrt(ms + RMS_EPS) * nw_ref[...] * _silu(r_ref[gi, rows, :])

    st_ref[...] = jnp.zeros(st_ref.shape, F32)
    _scan_loops(NL, NC, G, chunk, finish)


def _scan_call(kernel_fn, name, arrays, nw, ones_bd, state_cols, *, NL, NC):
    B, TT, _ = arrays[0].shape
    G = SCAN_G if B % SCAN_G == 0 else 1
    blk = lambda c: pl.BlockSpec((G, TT, c), lambda b: (b, 0, 0), pipeline_mode=pl.Buffered(1))
    return pl.pallas_call(
        functools.partial(kernel_fn, NL=NL, NC=NC, G=G),
        grid=(B // G,),
        in_specs=[blk(a.shape[-1]) for a in arrays] + [pl.BlockSpec((1, 256), lambda b: (0, 0)),
                                                        pl.BlockSpec((256, 256), lambda b: (0, 0))],
        out_specs=pl.BlockSpec((G, TT, 256), lambda b: (b, 0, 0)),
        out_shape=jax.ShapeDtypeStruct((B, TT, 256), F32),
        scratch_shapes=[pltpu.VMEM((G, 2, TT, 256), F32), pltpu.VMEM((G, 2, 256, state_cols), F32)],
        compiler_params=_cparams(("arbitrary",)),
        name=name,
    )(*arrays, nw, ones_bd)


def _gla_scan(gqk, gv, gr, gg, nw, ones_bd, *, NL, NC):
    return _scan_call(_gla_kernel, "gla_scan", (gqk, gv, gr, gg), nw, ones_bd, 128, NL=NL, NC=NC)


def _gdn_kernel(q_ref, k_ref, v_ref, z_ref, g_ref, nw_ref, ones_ref, o_ref, od_ref, s_ref, *, NL, NC, G):
    C = CHUNK
    tri = _tri_pair()
    ri, ci = _cat_masks()
    m_incl = [ci <= ri, ci >= ri]
    m_strict = [ci < ri, ci > ri]
    dmask = ci == ri
    eye_cat = dmask.astype(F32)
    bd = (_iota((256, 256), 0) >> 6) == (_iota((256, 256), 1) >> 6)
    ones64 = jnp.ones((C, C), BF16)
    lane_r = _iota((128, 256), 0)
    head_c = _iota((128, 256), 1) >> 6
    eg = [(lane_r == head_c + 4 * d).astype(BF16) for d in range(2)]
    eb = [(lane_r == head_c + 8 + 4 * d).astype(BF16) for d in range(2)]

    def block_diag(y):
        return jnp.where(bd, jnp.concatenate([y.astype(BF16)] * 4, axis=0), jnp.zeros((), BF16))

    def mm_cat(x, y):
        return _dot(x.astype(BF16), block_diag(y))

    def pair_mask(ls):
        return jnp.logical_and((ri >> (ls + 1)) == (ci >> (ls + 1)), (ri >> ls) != (ci >> ls))

    def chunk(chains):
        n = range(len(chains))
        G_ = [c[0] for c in chains]
        D_ = [c[1] for c in chains]
        R_ = [pl.ds(c[2], C) for c in chains]
        last = [C - 1 if d == 0 else 0 for d in D_]
        q = [q_ref[G_[j], R_[j], :] for j in n]
        k = [k_ref[G_[j], R_[j], :] for j in n]
        v = [v_ref[G_[j], R_[j], :] for j in n]
        gt = [g_ref[G_[j], R_[j], :] for j in n]
        cum = [_sel_dot(tri[D_[j]], gt[j]) for j in n]
        cexp = [_dot_sel(cum[j], eg[D_[j]]) for j in n]
        ct = [_sel_dot(ones64, jnp.where(dmask, cexp[j], 0.0)) for j in n]
        decay = [jnp.exp(jnp.where(m_incl[D_[j]], cexp[j] - ct[j], -jnp.inf)) for j in n]
        bexp = [_dot_sel(gt[j], eb[D_[j]]) for j in n]
        kb = [k[j] * bexp[j] for j in n]
        kbd = [block_diag(k[j]) for j in n]
        a = [jnp.where(m_strict[D_[j]], _dot_nt(kb[j].astype(BF16), kbd[j]) * decay[j], 0.0) for j in n]
        t = [eye_cat - jnp.where(pair_mask(0), a[j], 0.0) for j in n]
        for ls in range(1, 6):
            ta = [mm_cat(t[j], jnp.where(pair_mask(ls), a[j], 0.0)) for j in n]
            t = [t[j] - mm_cat(ta[j], t[j]) for j in n]
        ecum = [jnp.exp(cexp[j]) for j in n]
        u = [mm_cat(t[j], v[j] * bexp[j]) for j in n]
        w = [mm_cat(t[j], kb[j] * ecum[j]) for j in n]
        s = [s_ref[G_[j], D_[j]] for j in n]
        sb = [s[j].astype(BF16) for j in n]
        v_new = [u[j] - _dot(w[j].astype(BF16), sb[j]) for j in n]
        attn = [jnp.where(m_incl[D_[j]], _dot_nt(q[j].astype(BF16), kbd[j]) * decay[j], 0.0) for j in n]
        o = [_dot((q[j] * ecum[j]).astype(BF16), sb[j]) + mm_cat(attn[j], v_new[j]) for j in n]
        cl = [cexp[j][last[j]:last[j] + 1] for j in n]
        kd = [(k[j] * jnp.exp(cl[j] - cexp[j])).astype(BF16) for j in n]
        upd = [_dot_tn(kd[j], v_new[j].astype(BF16)) for j in n]
        for j in n:
            od_ref[G_[j], D_[j], R_[j], :] = o[j]
            s_ref[G_[j], D_[j]] = s[j] * jnp.exp(cl[j]) + jnp.where(bd, upd[j], 0.0)

    def finish(gi, rows):
        o = od_ref[gi, 0, rows, :] + od_ref[gi, 1, rows, :]
        ms = _dot_sel(o * o, ones_ref[...]) * (1.0 / 64.0)
        o_ref[gi, rows, :] = o * lax.rsqrt(ms + RMS_EPS) * nw_ref[...] * _silu(z_ref[gi, rows, :])

    s_ref[...] = jnp.zeros(s_ref.shape, F32)
    _scan_loops(NL, NC, G, chunk, finish)


def _layer_norm(t, g, b):
    tc = t - jnp.mean(t, axis=-1, keepdims=True)
    var = jnp.mean(tc * tc, axis=-1, keepdims=True)
    return tc * lax.rsqrt(var + LN_EPS) * g + b


def _outproj_kernel(yal_ref, yac_ref, yb_ref, yc_ref, yd_ref, x_ref, g1_ref, scp_ref, sh_ref, w_ref, lg_ref, lb_ref,
                    wr_ref, br_ref, x1_ref, h_ref, ti_ref, tw_ref, *, alpha, n_lat_tiles):
    ya = jnp.where(pl.program_id(1) == n_lat_tiles, yac_ref[0], yal_ref[0])
    mix = (_dot(ya.astype(BF16), w_ref[0:GW]) + _dot(yb_ref[0].astype(BF16), w_ref[GW:2 * GW])
           + _dot(yc_ref[0].astype(BF16), w_ref[2 * GW:3 * GW]) + _dot(yd_ref[0].astype(BF16), w_ref[3 * GW:4 * GW]))
    x1 = _layer_norm(alpha * x_ref[0] + g1_ref[0] * mix, lg_ref[...], lb_ref[...])
    x1_ref[0] = x1
    h = x1 * scp_ref[0] + sh_ref[0]
    for s in range(8):
        h_ref[0, :, s, :] = h[:, s * 128:(s + 1) * 128]
    logits = jnp.dot(h, wr_ref[...], precision=HI, preferred_element_type=F32) + br_ref[...]
    lane = _iota(logits.shape, 1)
    vals, idxs = [], []
    for _ in range(TOP_K):
        m = jnp.max(logits, axis=-1, keepdims=True)
        idx = jnp.min(jnp.where(logits == m, lane, 128), axis=-1, keepdims=True)
        vals.append(m)
        idxs.append(idx)
        logits = jnp.where(lane == idx, -jnp.inf, logits)
    ex = [jnp.exp(v - vals[0]) for v in vals]
    inv = 1.0 / (ex[0] + ex[1] + ex[2] + ex[3])
    ti = jnp.zeros(lane.shape, I32)
    tw = jnp.zeros(lane.shape, F32)
    for kk in range(TOP_K):
        ti = jnp.where(lane == kk, idxs[kk], ti)
        tw = jnp.where(lane == kk, ex[kk] * inv, tw)
    ti_ref[0] = ti
    tw_ref[0] = tw


def _outproj(ya_lat, ya_ctx, yb, yc, yd, xcat, g1, scp, sh, w_bf16, lg, lb, wr, br, *, NL, NC, alpha):
    B, TT, _ = xcat.shape
    TR = NC
    n_lat_tiles = NL // TR
    mod_idx = lambda b, i: (jnp.where(i == n_lat_tiles, B, b), 0, 0)
    const2 = lambda b, i: (0, 0)
    row_blk = lambda c: pl.BlockSpec((1, TR, c), lambda b, i: (b, i, 0))
    mod = lambda: pl.BlockSpec((1, 1, D), mod_idx)
    return pl.pallas_call(
        functools.partial(_outproj_kernel, alpha=alpha, n_lat_tiles=n_lat_tiles),
        grid=(B, TT // TR),
        in_specs=[pl.BlockSpec((1, TR, GW), lambda b, i: (b, jnp.minimum(i, n_lat_tiles - 1), 0)),
                  pl.BlockSpec((1, TR, GW), lambda b, i: (b, 0, 0)),
                  row_blk(GW), row_blk(GW), row_blk(GW), row_blk(D), mod(), mod(), mod(),
                  pl.BlockSpec((D, D), const2), pl.BlockSpec((1, D), const2), pl.BlockSpec((1, D), const2),
                  pl.BlockSpec((D, 128), const2), pl.BlockSpec((1, 128), const2)],
        out_specs=[row_blk(D), pl.BlockSpec((1, TR, 8, 128), lambda b, i: (b, i, 0, 0)), row_blk(128), row_blk(128)],
        out_shape=[jax.ShapeDtypeStruct((B, TT, D), F32), jax.ShapeDtypeStruct((B, TT, 8, 128), F32),
                   jax.ShapeDtypeStruct((B, TT, 128), I32), jax.ShapeDtypeStruct((B, TT, 128), F32)],
        compiler_params=_cparams(("arbitrary", "arbitrary")),
        name="outproj_ln_router",
    )(ya_lat, ya_ctx, yb, yc, yd, xcat, g1, scp, sh, w_bf16, lg, lb, wr, br)


def _route(tidx, tw, tm):
    B, S, _ = tidx.shape
    e = tidx.reshape(B, S * TOP_K)
    order = jnp.argsort(e, axis=1, stable=True).astype(I32)
    tws = jnp.take_along_axis(tw.reshape(B, S * TOP_K), order, axis=1)
    counts = jnp.sum((e[:, :, None] == jnp.arange(N_EXP, dtype=I32)[None, None, :]).astype(I32), axis=1)
    starts = jnp.cumsum(counts, axis=1) - counts
    pad = ((0, 0), (0, tm))
    return (jnp.pad(order, pad)[:, None, :], jnp.pad(tws, pad)[:, None, :],
            counts.reshape(-1).astype(I32), starts.reshape(-1).astype(I32))


def _expert_kernel(cnt_ref, st_ref, ord_ref, tws_ref, h_ref, wg_ref, bg_ref, wu_ref, bu_ref, wd_ref, bd_ref,
                   acc_ref, xbuf, ybuf, *, tm):
    s = pl.program_id(0)
    e = pl.program_id(1)
    n = cnt_ref[s * N_EXP + e]
    st = st_ref[s * N_EXP + e]

    @pl.when(e == 0)
    def _():
        acc_ref[...] = jnp.zeros(acc_ref.shape, F32)

    def tile(j, carry):
        base = st + j * tm
        m = n - j * tm

        def gather(r, c):
            xbuf[r] = h_ref[ord_ref[0, 0, base + r] >> 2]
            return c
        lax.fori_loop(0, tm, gather, 0, unroll=8)

        x = jnp.concatenate([xbuf[:, q, :] for q in range(8)], axis=1).astype(BF16)
        gate = jnp.minimum(_dot(x, wg_ref[0, 0]) + bg_ref[0, 0], SWIGLU_LIMIT)
        up = jnp.clip(_dot(x, wu_ref[0, 0]) + bu_ref[0, 0], -SWIGLU_LIMIT, SWIGLU_LIMIT)
        act = (up + 1.0) * gate * _sigmoid(SWIGLU_ALPHA * gate)
        y = _dot(act.astype(BF16), wd_ref[0, 0]) + bd_ref[0, 0]
        for q in range(8):
            ybuf[:, q, :] = y[:, q * 128:(q + 1) * 128]

        def scatter(r, c):
            tok = ord_ref[0, 0, base + r] >> 2
            w = jnp.where(r < m, tws_ref[0, 0, base + r], 0.0)
            acc_ref[tok] = acc_ref[tok] + w * ybuf[r]
            return c
        lax.fori_loop(0, tm, scatter, 0, unroll=8)
        return carry

    lax.fori_loop(0, (n + tm - 1) // tm, tile, 0)


def _experts(h4d, order, tws, counts, starts, wg, bg, wu, bu, wd, bd, *, tm, layer):
    B, S = h4d.shape[:2]
    n_ord = order.shape[-1]
    wspec = lambda: pl.BlockSpec((1, 1, D, D), lambda s, e, c, t: (layer, e, 0, 0))
    bspec = lambda: pl.BlockSpec((1, 1, 1, D), lambda s, e, c, t: (layer, e, 0, 0))
    smem = lambda: pl.BlockSpec((1, 1, n_ord), lambda s, e, c, t: (s, 0, 0), memory_space=pltpu.SMEM)
    tok = lambda **kw: pl.BlockSpec((None, S, 8, 128), lambda s, e, c, t: (s, 0, 0, 0), **kw)
    return pl.pallas_call(
        functools.partial(_expert_kernel, tm=tm),
        grid_spec=pltpu.PrefetchScalarGridSpec(
            num_scalar_prefetch=2,
            grid=(B, N_EXP),
            in_specs=[smem(), smem(), tok(pipeline_mode=pl.Buffered(1)),
                      wspec(), bspec(), wspec(), bspec(), wspec(), bspec()],
            out_specs=tok(),
            scratch_shapes=[pltpu.VMEM((tm, 8, 128), F32), pltpu.VMEM((tm, 8, 128), F32)]),
        out_shape=jax.ShapeDtypeStruct((B, S, 8, 128), F32),
        compiler_params=_cparams(("arbitrary", "arbitrary")),
        name="moe_experts",
    )(counts, starts, order, tws, h4d, wg, bg, wu, bu, wd, bd)


def _combine_kernel(f_ref, x_ref, g2_ref, lg_ref, lb_ref, o_ref, *, alpha):
    f = jnp.concatenate([f_ref[0, :, q, :] for q in range(8)], axis=1)
    o_ref[0] = _layer_norm(alpha * x_ref[0] + g2_ref[0] * f, lg_ref[...], lb_ref[...])


def _combine(f4d, x1, g2, lg, lb, *, NL, NC, alpha):
    B, TT, _ = x1.shape
    TR = NC
    n_lat_tiles = NL // TR
    mod_idx = lambda b, i: (jnp.where(i == n_lat_tiles, B, b), 0, 0)
    return pl.pallas_call(
        functools.partial(_combine_kernel, alpha=alpha),
        grid=(B, TT // TR),
        in_specs=[pl.BlockSpec((1, TR, 8, 128), lambda b, i: (b, i, 0, 0)),
                  pl.BlockSpec((1, TR, D), lambda b, i: (b, i, 0)), pl.BlockSpec((1, 1, D), mod_idx),
                  pl.BlockSpec((1, D), lambda b, i: (0, 0)), pl.BlockSpec((1, D), lambda b, i: (0, 0))],
        out_specs=pl.BlockSpec((1, TR, D), lambda b, i: (b, i, 0)),
        out_shape=jax.ShapeDtypeStruct((B, TT, D), F32),
        compiler_params=_cparams(("arbitrary", "arbitrary")),
        name="moe_combine_ln",
    )(f4d, x1, g2, lg, lb)


def _gdn_scan(dq, dk, dv, dz, dg, nw, ones_bd, *, NL, NC):
    return _scan_call(_gdn_kernel, "gdn_scan", (dq, dk, dv, dz, dg), nw, ones_bd, 256, NL=NL, NC=NC)


def _expert_tile_rows(tokens_per_sample):
    mean = tokens_per_sample * TOP_K / N_EXP
    return max(32, int(-(-(mean * 10 / 9) // 32) * 32))


def kernel(x, c, ctx, c_ctx, w_ada, b_ada, w_in, hy_conv_w, hy_conv_b, hy_w1, hy_b1, hy_w2, hy_b2, hy_w3, hy_b3, hy_w4, hy_freq, hy_d, sc_conv_w, gla_w_a, gla_b_a, gla_norm_w, gdn_conv_w, gdn_a_log, gdn_dt_bias, gdn_norm_w, w_out, ln1_g, ln1_b, w_router, b_router, w_gate, b_gate, w_up, b_up, w_down, b_down, ln2_g, ln2_b):
    B, NL, _ = x.shape
    NC = ctx.shape[1]
    TT = NL + NC
    T = B * TT
    depth = w_ada.shape[0]
    alpha = (2 * depth) ** 0.25
    assert NL % NC == 0 and NC % CHUNK == 0 and (NC & (NC - 1)) == 0

    xcat = jnp.concatenate([x, ctx], axis=1)
    n_mod = -(-(B + 1) // 8) * 8
    cc = jnp.concatenate([c, c_ctx[None], jnp.zeros((n_mod - B - 1, D), F32)], axis=0)
    mods = _ada_mods(cc, w_ada, b_ada)

    pp = _prep_inproj_params(w_in, hy_conv_w, hy_conv_b, hy_d, sc_conv_w, gla_w_a, gla_b_a, gdn_conv_w,
                             gdn_a_log, gdn_dt_bias)
    ones_bd = _block_ones(256, 64)
    hy_args = (jnp.pad(hy_w1, ((0, 0), (0, HY_HID - HY_EMB), (0, 0))), hy_b1[:, None], hy_w2, hy_b2[:, None],
               hy_w3, hy_b3[:, None], hy_w4, hy_freq)
    tabs_l = _dft_tables(NL)
    tabs_c = _dft_tables(NC)
    spec_l = _hyena_spectrum(NL, tabs_l[0], tabs_l[1], *hy_args)
    spec_c = _hyena_spectrum(NC, tabs_c[0], tabs_c[1], *hy_args)
    tabs_l = tuple(t.astype(BF16) for t in tabs_l)
    tabs_c = tuple(t.astype(BF16) for t in tabs_c)

    w_out_b = w_out.astype(BF16)
    wr = _pad_cols(w_router, 128)
    br = jnp.pad(b_router, ((0, 0), (0, 128 - N_EXP)), constant_values=NEG_BIG)[:, None, :]
    wg, wu, wd = w_gate.astype(BF16), w_up.astype(BF16), w_down.astype(BF16)
    tm = _expert_tile_rows(TT)

    for l in range(depth):
        sh1, sc1, g1, sh2, sc2, g2 = (mods[l, :, k * D:(k + 1) * D][:, None, :] for k in range(6))
        zb, e, x0, ysc, gqk, gv, gr, gg, dq, dk, dv, dz, dg = _inproj(
            xcat, 1.0 + sc1, sh1, pp["w"][l], pp["hyw"][l], pp["hyb"][l], pp["hyd"][l], pp["scw"][l],
            pp["glaw"][l], pp["glab"][l], pp["gdnw"][l], pp["gdna"][l], pp["gdnd"][l], ones_bd, NL=NL, NC=NC)
        ya_lat = _hyconv(zb, e, x0, spec_l[l:l + 1], tabs_l, L=NL, row_blk=0)
        ya_ctx = _hyconv(zb, e, x0, spec_c[l:l + 1], tabs_c, L=NC, row_blk=NL // NC)
        yc = _gla_scan(gqk, gv, gr, gg, jnp.tile(gla_norm_w[l], 4)[None], ones_bd, NL=NL, NC=NC)
        yd = _gdn_scan(dq, dk, dv, dz, dg, jnp.tile(gdn_norm_w[l], 4)[None], ones_bd, NL=NL, NC=NC)
        x1, h, ti, tw = _outproj(ya_lat, ya_ctx, ysc, yc, yd, xcat, g1, 1.0 + sc2, sh2, w_out_b[l], ln1_g[l][None],
                                 ln1_b[l][None], wr[l], br[l], NL=NL, NC=NC, alpha=alpha)
        order, tws, counts, starts = _route(ti[:, :, :TOP_K], tw[:, :, :TOP_K], tm)
        f = _experts(h, order, tws, counts, starts, wg, b_gate[:, :, None, :], wu, b_up[:, :, None, :], wd,
                     b_down[:, :, None, :], tm=tm, layer=l)
        xcat = _combine(f, x1, g2, ln2_g[l][None], ln2_b[l][None], NL=NL, NC=NC, alpha=alpha)
    return xcat[:, :NL]
```

```python
import functools
import math

import numpy as np
import jax
import jax.numpy as jnp
from jax import lax
from jax.experimental import pallas as pl
from jax.experimental.pallas import tpu as pltpu

F32 = jnp.float32
BF16 = jnp.bfloat16
I32 = jnp.int32
HI = lax.Precision.HIGHEST

D = 1024
GW = 256
CHUNK = 64
ROW_W = 64
GLA_DK = 32
GLA_TAU = 16.0
GDN_DK = 64
N_EXP = 32
TOP_K = 4
SWIGLU_LIMIT = 7.0
SWIGLU_ALPHA = 1.702
LN_EPS = 1e-5
RMS_EPS = 1e-6
L2_EPS = 1e-6
HY_EMB = 33
HY_HID = 64
NEG_BIG = -1e30
P_HY, P_SC, P_GLA, P_GDN = 768, 768, 896, 1152
P_IN = P_HY + P_SC + P_GLA + P_GDN
VMEM_LIMIT = 56 * 1024 * 1024


def _cparams(sem):
    return pltpu.CompilerParams(dimension_semantics=sem, vmem_limit_bytes=VMEM_LIMIT)


def _split_bf16(x):
    hi = x.astype(BF16)
    lo = (x - hi.astype(F32)).astype(BF16)
    return hi, lo


def _dot(a, b):
    return jnp.dot(a, b, preferred_element_type=F32)


def _dot_nt(a, b):
    return lax.dot_general(a, b, (((1,), (1,)), ((), ())), preferred_element_type=F32)


def _dot_tn(a, b):
    return lax.dot_general(a, b, (((0,), (0,)), ((), ())), preferred_element_type=F32)


def _dot_sel(x, sel_bf16):
    h1 = x.astype(BF16)
    r1 = x - h1.astype(F32)
    h2 = r1.astype(BF16)
    h3 = (r1 - h2.astype(F32)).astype(BF16)
    return _dot(h1, sel_bf16) + _dot(h2, sel_bf16) + _dot(h3, sel_bf16)


def _sel_dot(sel_bf16, x):
    h1 = x.astype(BF16)
    r1 = x - h1.astype(F32)
    h2 = r1.astype(BF16)
    h3 = (r1 - h2.astype(F32)).astype(BF16)
    return _dot(sel_bf16, h1) + _dot(sel_bf16, h2) + _dot(sel_bf16, h3)


def _iota(shape, dim):
    return lax.broadcasted_iota(I32, shape, dim)


def _sigmoid(x):
    return 1.0 / (1.0 + jnp.exp(-x))


def _silu(x):
    return x * _sigmoid(x)


def _softplus(x):
    return jnp.maximum(x, 0.0) + jnp.log(1.0 + jnp.exp(-jnp.abs(x)))


def _log_sigmoid(x):
    return -_softplus(-x)


def _ada_kernel(cc_ref, w_ref, b_ref, o_ref):
    o_ref[0] = jnp.dot(_silu(cc_ref[...]), w_ref[0], precision=HI, preferred_element_type=F32) + b_ref[0]


def _ada_mods(cc, w_ada, b_ada):
    depth, _, n6 = w_ada.shape
    rows = cc.shape[0]
    tn = 1536
    return pl.pallas_call(
        _ada_kernel,
        grid=(depth, n6 // tn),
        in_specs=[pl.BlockSpec((rows, D), lambda l, j: (0, 0)),
                  pl.BlockSpec((1, D, tn), lambda l, j: (l, 0, j)),
                  pl.BlockSpec((1, 1, tn), lambda l, j: (l, 0, j))],
        out_specs=pl.BlockSpec((1, rows, tn), lambda l, j: (l, 0, j)),
        out_shape=jax.ShapeDtypeStruct((depth, rows, n6), F32),
        compiler_params=_cparams(("arbitrary", "arbitrary")),
        name="ada_mods",
    )(cc, w_ada, b_ada.reshape(depth, 1, n6))


def _conv3(x, w, rowlen):
    n = x.shape[0]
    pos = _iota(x.shape, 0) & (rowlen - 1)
    xm = jnp.where(pos == 0, 0.0, pltpu.roll(x, 1, axis=0))
    xp = jnp.where(pos == rowlen - 1, 0.0, pltpu.roll(x, n - 1, axis=0))
    return xm * w[0:1] + x * w[1:2] + xp * w[2:3]


def _inproj_kernel(x_ref, scp_ref, sh_ref, w_ref, hyw_ref, hyb_ref, hyd_ref, scw_ref, glaw_ref, glab_ref,
                   gdnw_ref, gdna_ref, gdnd_ref, ones_ref,
                   zb_ref, e_ref, x0_ref, ysc_ref, gqk_ref, gv_ref, gr_ref, gg_ref,
                   dq_ref, dk_ref, dv_ref, dz_ref, dg_ref, *, n_lat_tiles, ctx_len):
    i = pl.program_id(1)
    rowlen = jnp.where(i == n_lat_tiles, ctx_len, ROW_W)
    xm = (x_ref[0] * scp_ref[0] + sh_ref[0]).astype(BF16)

    u = _conv3(_dot(xm, w_ref[:, 0:P_HY]), hyw_ref[...], rowlen) + hyb_ref[...]
    z = u[:, GW:2 * GW] * u[:, 2 * GW:3 * GW]
    zb_ref[0] = z.astype(BF16)
    e_ref[0] = z * hyd_ref[...]
    x0_ref[0] = u[:, 0:GW]

    ps = _dot(xm, w_ref[:, P_HY:P_HY + P_SC])
    ysc_ref[0] = ps[:, 0:GW] * _conv3(ps[:, GW:2 * GW] * ps[:, 2 * GW:3 * GW], scw_ref[...], rowlen)

    pg = _dot(xm, w_ref[:, P_HY + P_SC:P_HY + P_SC + P_GLA])
    gqk_ref[0] = pg[:, 0:256]
    gv_ref[0] = pg[:, 256:512]
    gr_ref[0] = pg[:, 512:768]
    ah, al = _split_bf16(pg[:, 768:896])
    wh, wl = _split_bf16(glaw_ref[...])
    logit = _dot(ah, wh) + _dot(al, wh) + _dot(ah, wl) + glab_ref[...]
    gg_ref[0] = _log_sigmoid(logit) * (1.0 / GLA_TAU)

    o3 = P_HY + P_SC + P_GLA
    pd = _dot(xm, w_ref[:, o3:o3 + P_GDN])
    qkv = _silu(_conv3(pd[:, 0:768], gdnw_ref[...], rowlen))
    ones_bd = ones_ref[...]
    q = qkv[:, 0:256]
    k = qkv[:, 256:512]
    dq_ref[0] = q * lax.rsqrt(_dot_sel(q * q, ones_bd) + L2_EPS) * (GDN_DK ** -0.5)
    dk_ref[0] = k * lax.rsqrt(_dot_sel(k * k, ones_bd) + L2_EPS)
    dv_ref[0] = qkv[:, 512:768]
    dz_ref[0] = pd[:, 768:1024]
    gl = pd[:, 1024:1152]
    lane = _iota(gl.shape, 1)
    dg_ref[0] = jnp.where(lane < 8, gdna_ref[...] * _softplus(gl + gdnd_ref[...]), _sigmoid(gl))


def _inproj(xcat, scp, sh, w_bf16, hyw, hyb, hyd, scw, glaw, glab, gdnw, gdna, gdnd, ones_bd, *, NL, NC):
    B, TT, _ = xcat.shape
    TR = NC
    n_lat_tiles = NL // TR
    nt = TT // TR
    mod_idx = lambda b, i: (jnp.where(i == n_lat_tiles, B, b), 0, 0)
    const2 = lambda b, i: (0, 0)
    row_blk = lambda c: pl.BlockSpec((1, TR, c), lambda b, i: (b, i, 0))
    outs = [(GW, BF16)] + [(GW, F32)] * 11 + [(128, F32)]
    return pl.pallas_call(
        functools.partial(_inproj_kernel, n_lat_tiles=n_lat_tiles, ctx_len=NC),
        grid=(B, nt),
        in_specs=[row_blk(D),
                  pl.BlockSpec((1, 1, D), mod_idx), pl.BlockSpec((1, 1, D), mod_idx),
                  pl.BlockSpec((D, P_IN), const2),
                  pl.BlockSpec((3, P_HY), const2), pl.BlockSpec((1, P_HY), const2), pl.BlockSpec((1, GW), const2),
                  pl.BlockSpec((3, GW), const2),
                  pl.BlockSpec((128, 256), const2), pl.BlockSpec((1, 256), const2),
                  pl.BlockSpec((3, 768), const2), pl.BlockSpec((1, 128), const2), pl.BlockSpec((1, 128), const2),
                  pl.BlockSpec((256, 256), const2)],
        out_specs=[row_blk(c) for c, _ in outs],
        out_shape=[jax.ShapeDtypeStruct((B, TT, c), dt) for c, dt in outs],
        compiler_params=_cparams(("arbitrary", "arbitrary")),
        name="inproj_local",
    )(xcat, scp, sh, w_bf16, hyw, hyb, hyd, scw, glaw, glab, gdnw, gdna, gdnd, ones_bd)


def _dft_tables(L):
    N = 2 * L
    f = jnp.arange(L, dtype=I32)[:, None]
    t = jnp.arange(L, dtype=I32)[None, :]
    ang = ((f * t) % N).astype(F32) * (2.0 * math.pi / N)
    cos = jnp.cos(ang)
    sin = jnp.sin(ang)
    nyq = jnp.where(t % 2 == 0, 1.0, -1.0).astype(F32)
    mc = cos
    ms = jnp.where(f == 0, nyq, -sin)
    wf = jnp.where(f == 0, 1.0, 2.0).astype(F32) / N
    ic = (cos * wf).T
    isn = jnp.where(f == 0, nyq / N, -sin * wf).T
    return mc, ms, ic, isn


def _hyfilt_kernel(zf_ref, win_ref, w1_ref, b1_ref, w2_ref, b2_ref, w3_ref, b3_ref, w4_ref, fr_ref,
                   mc_ref, ms_ref, o_ref, ks_ref):
    j = pl.program_id(1)

    @pl.when(j == 0)
    def _():
        dot = lambda a, b: jnp.dot(a, b, precision=HI, preferred_element_type=F32)
        h = jnp.sin(fr_ref[0, 0:1] * (dot(zf_ref[...], w1_ref[0]) + b1_ref[0]))
        h = jnp.sin(fr_ref[0, 1:2] * (dot(h, w2_ref[0]) + b2_ref[0]))
        h = jnp.sin(fr_ref[0, 2:3] * (dot(h, w3_ref[0]) + b3_ref[0]))
        k = dot(h, w4_ref[0])
        win = win_ref[...]
        kf = k[:, 0:GW] * win
        kb = k[:, GW:2 * GW] * win
        kb = jnp.where(_iota(kb.shape, 0) == 0, 0.0, kb)
        l1 = jnp.sum(jnp.abs(kf), axis=0, keepdims=True) + jnp.sum(jnp.abs(kb), axis=0, keepdims=True)
        kf = kf / l1
        kb = kb / l1
        ks_ref[:, 0:GW] = kf + kb
        ks_ref[:, GW:2 * GW] = kf - kb

    ks = ks_ref[...]
    o_ref[0, 0] = jnp.dot(mc_ref[...], ks, precision=HI, preferred_element_type=F32)
    o_ref[0, 1] = jnp.dot(ms_ref[...], ks, precision=HI, preferred_element_type=F32)


def _hyena_spectrum(L, mc, ms, w1p, b1, w2, b2, w3, b3, w4, freq):
    depth = w1p.shape[0]
    t = jnp.linspace(0.0, 1.0, L, dtype=F32)[:, None]
    bands = (HY_EMB - 1) // 2
    ang = 2.0 * math.pi * jnp.arange(L, dtype=F32)[:, None] / L
    f = jnp.linspace(1e-4, bands - 1, bands, dtype=F32)[None, :]
    zf = jnp.concatenate([t, jnp.cos(f * ang), -jnp.sin(f * ang), jnp.zeros((L, HY_HID - HY_EMB), F32)], axis=-1)
    max_decay = math.log(1e-2) / 0.3
    min_decay = math.log(1e-2) / 1.5
    deltas = jnp.abs(jnp.linspace(min_decay, max_decay, GW, dtype=F32))
    win = jnp.exp(-t * deltas[None, :])
    ft = min(L, 512)
    c2 = lambda l, j: (0, 0)
    lw = lambda *s: pl.BlockSpec((1,) + s, lambda l, j: (l,) + (0,) * len(s))
    a = pl.pallas_call(
        _hyfilt_kernel,
        grid=(depth, L // ft),
        in_specs=[pl.BlockSpec((L, HY_HID), c2), pl.BlockSpec((L, GW), c2),
                  lw(HY_HID, HY_HID), lw(1, HY_HID), lw(HY_HID, HY_HID), lw(1, HY_HID),
                  lw(HY_HID, HY_HID), lw(1, HY_HID), lw(HY_HID, 2 * GW), lw(3, HY_HID),
                  pl.BlockSpec((ft, L), lambda l, j: (j, 0)), pl.BlockSpec((ft, L), lambda l, j: (j, 0))],
        out_specs=pl.BlockSpec((1, 2, ft, 2 * GW), lambda l, j: (l, 0, j, 0)),
        out_shape=jax.ShapeDtypeStruct((depth, 2, L, 2 * GW), F32),
        scratch_shapes=[pltpu.VMEM((L, 2 * GW), F32)],
        compiler_params=_cparams(("arbitrary", "arbitrary")),
        name="hyena_filter_spectrum",
    )(zf, win, w1p, b1, w2, b2, w3, b3, w4, freq, mc, ms)
    k_re = a[:, 0, :, 0:GW]
    k_im = a[:, 1, :, GW:2 * GW]
    k_im = k_im.at[:, 0, :].set(a[:, 1, 0, 0:GW])
    return jnp.stack([k_re, k_im], axis=1)


def _hyconv_kernel(zb_ref, e_ref, x0_ref, k_ref, mc_ref, ms_ref, ic_ref, is_ref, o_ref, acc_ref, *, G):
    j = pl.program_id(1)
    nj = pl.num_programs(1)
    kt = k_ref[0, 0]
    kb = k_ref[0, 1]
    row0 = jnp.logical_and(_iota(kt.shape, 0) == 0, j == 0)
    mc = mc_ref[...]
    ms = ms_ref[...]
    ic = ic_ref[...]
    isn = is_ref[...]
    for g in range(G):
        z = zb_ref[g]
        zt = _dot(mc, z)
        zi = _dot(ms, z)
        yt = zt * kt - jnp.where(row0, 0.0, zi * kb)
        yi = jnp.where(row0, zi * kb, zt * kb + zi * kt)
        contrib = _dot(ic, yt.astype(BF16)) + _dot(isn, yi.astype(BF16))

        @pl.when(j == 0)
        def _():
            acc_ref[g] = contrib

        @pl.when(j > 0)
        def _():
            acc_ref[g] += contrib

    @pl.when(j == nj - 1)
    def _():
        for g in range(G):
            o_ref[g] = (acc_ref[g] + e_ref[g]) * x0_ref[g]


def _hyconv(zb, e, x0, kspec_l, tabs, *, L, row_blk):
    B = zb.shape[0]
    G = 2 if B % 2 == 0 else 1
    ft = min(L, 256)
    mc, ms, ic, isn = tabs
    seg = lambda: pl.BlockSpec((G, L, GW), lambda b, j: (b, row_blk, 0))
    return pl.pallas_call(
        functools.partial(_hyconv_kernel, G=G),
        grid=(B // G, L // ft),
        in_specs=[seg(), seg(), seg(),
                  pl.BlockSpec((1, 2, ft, GW), lambda b, j: (0, 0, j, 0)),
                  pl.BlockSpec((ft, L), lambda b, j: (j, 0)), pl.BlockSpec((ft, L), lambda b, j: (j, 0)),
                  pl.BlockSpec((L, ft), lambda b, j: (0, j)), pl.BlockSpec((L, ft), lambda b, j: (0, j))],
        out_specs=pl.BlockSpec((G, L, GW), lambda b, j: (b, 0, 0)),
        out_shape=jax.ShapeDtypeStruct((B, L, GW), F32),
        scratch_shapes=[pltpu.VMEM((G, L, GW), F32)],
        compiler_params=_cparams(("arbitrary", "arbitrary")),
        name=f"hyena_longconv_{L}",
    )(zb, e, x0, kspec_l, mc, ms, ic, isn)


def _pad_cols(a, n):
    return jnp.pad(a, [(0, 0)] * (a.ndim - 1) + [(0, n - a.shape[-1])])


def _prep_inproj_params(w_in, hy_conv_w, hy_conv_b, hy_d, sc_conv_w, gla_w_a, gla_b_a, gdn_conv_w, gdn_a_log,
                        gdn_dt_bias):
    depth = w_in.shape[0]
    o1, o2, o3 = 768, 1536, 1536 + 800
    w = jnp.concatenate([w_in[..., 0:o2], _pad_cols(w_in[..., o2:o3], P_GLA), _pad_cols(w_in[..., o3:], P_GDN)],
                        axis=-1).astype(BF16)
    glaw = jnp.zeros((depth, 128, 256), F32)
    glaw = glaw.at[:, 0:16, 0:128].set(gla_w_a[:, 0]).at[:, 16:32, 128:256].set(gla_w_a[:, 1])
    glab = gla_b_a.reshape(depth, 1, 256)
    gdna = _pad_cols(-jnp.exp(gdn_a_log.astype(F32)).reshape(depth, 1, 8), 128)
    gdnd = _pad_cols(gdn_dt_bias.astype(F32).reshape(depth, 1, 8), 128)
    return dict(w=w, hyw=hy_conv_w, hyb=hy_conv_b[:, None, :], hyd=hy_d[:, None, :], scw=sc_conv_w, glaw=glaw,
                glab=glab, gdnw=gdn_conv_w, gdna=gdna, gdnd=gdnd)


def _block_ones(n, blk):
    r = np.arange(n)
    return jnp.asarray((r[:, None] // blk) == (r[None, :] // blk), BF16)


def _tri_pair():
    r = _iota((CHUNK, CHUNK), 0)
    c = _iota((CHUNK, CHUNK), 1)
    return [(r >= c).astype(BF16), (r <= c).astype(BF16)]


def _cat_masks():
    r = _iota((CHUNK, 4 * CHUNK), 0)
    c = _iota((CHUNK, 4 * CHUNK), 1) & (CHUNK - 1)
    return r, c


SCAN_G = 2


def _scan_loops(NL, NC, G, chunk_fn, finish_fn):
    def phase(n, base0):
        def body(i, carry):
            bf = pl.multiple_of(base0 + i * CHUNK, CHUNK)
            bb = pl.multiple_of(base0 + (n - 1 - i) * CHUNK, CHUNK)
            chunk_fn([(g, d, (bf, bb)[d]) for g in range(G) for d in range(2)])
            return carry
        lax.fori_loop(0, n, body, 0)

    phase(NC // CHUNK, NL)
    phase(NL // CHUNK, 0)

    def fin(i, carry):
        base = pl.multiple_of(i * NC, NC)
        for g in range(G):
            finish_fn(g, pl.ds(base, NC))
        return carry
    lax.fori_loop(0, (NL + NC) // NC, fin, 0)


def _gla_kernel(qk_ref, v_ref, r_ref, g_ref, nw_ref, ones_ref, o_ref, od_ref, st_ref, *, NL, NC, G):
    C = CHUNK
    scale = GLA_DK ** -0.5
    tri = _tri_pair()
    ri, ci = _cat_masks()
    cmask = [ci <= ri, ci >= ri]
    hm = (_iota((256, 128), 0) >> 6) == (_iota((256, 128), 1) >> 5)
    vbd = (_iota((256, 256), 0) >> 6) == (_iota((256, 256), 1) >> 6)

    def chunk(chains):
        n = range(len(chains))
        G_ = [c[0] for c in chains]
        D_ = [c[1] for c in chains]
        R_ = [pl.ds(c[2], C) for c in chains]
        mid = [C // 2 if d == 0 else C // 2 - 1 for d in D_]
        last = [C - 1 if d == 0 else 0 for d in D_]
        q = [qk_ref[G_[j], R_[j], 0:128] * scale for j in n]
        k = [qk_ref[G_[j], R_[j], 128:256] for j in n]
        v = [v_ref[G_[j], R_[j], :].astype(BF16) for j in n]
        g = [g_ref[G_[j], R_[j], D_[j] * 128:(D_[j] + 1) * 128] for j in n]
        b = [_sel_dot(tri[D_[j]], g[j]) for j in n]
        b_mid = [b[j][mid[j]:mid[j] + 1] for j in n]
        b_last = [b[j][last[j]:last[j] + 1] for j in n]
        qe = [(q[j] * jnp.exp(b[j] - b_mid[j])).astype(BF16) for j in n]
        ke = [(k[j] * jnp.exp(b_mid[j] - b[j])).astype(BF16) for j in n]
        kbd = [jnp.where(hm, jnp.concatenate([ke[j]] * 4, axis=0), jnp.zeros((), BF16)) for j in n]
        a = [jnp.where(cmask[D_[j]], _dot_nt(qe[j], kbd[j]), 0.0).astype(BF16) for j in n]
        vb = [jnp.where(vbd, jnp.concatenate([v[j]] * 4, axis=0), jnp.zeros((), BF16)) for j in n]
        st = [st_ref[G_[j], D_[j]] for j in n]
        qb = [(q[j] * jnp.exp(b[j])).astype(BF16) for j in n]
        o = [_dot(a[j], vb[j]) + _dot_nt(qb[j], st[j].astype(BF16)) for j in n]
        kd = [(k[j] * jnp.exp(b_last[j] - b[j])).astype(BF16) for j in n]
        upd = [_dot_tn(v[j], kd[j]) for j in n]
        for j in n:
            od_ref[G_[j], D_[j], R_[j], :] = o[j]
            st_ref[G_[j], D_[j]] = st[j] * jnp.exp(b_last[j]) + jnp.where(hm, upd[j], 0.0)

    def finish(gi, rows):
        o = od_ref[gi, 0, rows, :] + od_ref[gi, 1, rows, :]
        ms = _dot_sel(o * o, ones_ref[...]) * (1.0 / 64.0)
        o_ref[gi, rows, :] = o * lax.rsqrt(ms + RMS_EPS) * nw_ref[...] * _silu(r_ref[gi, rows, :])

    st_ref[...] = jnp.zeros(st_ref.shape, F32)
    _scan_loops(NL, NC, G, chunk, finish)


def _scan_call(kernel_fn, name, arrays, nw, ones_bd, state_cols, *, NL, NC):
    B, TT, _ = arrays[0].shape
    G = SCAN_G if B % SCAN_G == 0 else 1
    blk = lambda c: pl.BlockSpec((G, TT, c), lambda b: (b, 0, 0), pipeline_mode=pl.Buffered(1))
    return pl.pallas_call(
        functools.partial(kernel_fn, NL=NL, NC=NC, G=G),
        grid=(B // G,),
        in_specs=[blk(a.shape[-1]) for a in arrays] + [pl.BlockSpec((1, 256), lambda b: (0, 0)),
                                                        pl.BlockSpec((256, 256), lambda b: (0, 0))],
        out_specs=pl.BlockSpec((G, TT, 256), lambda b: (b, 0, 0)),
        out_shape=jax.ShapeDtypeStruct((B, TT, 256), F32),
        scratch_shapes=[pltpu.VMEM((G, 2, TT, 256), F32), pltpu.VMEM((G, 2, 256, state_cols), F32)],
        compiler_params=_cparams(("arbitrary",)),
        name=name,
    )(*arrays, nw, ones_bd)


def _gla_scan(gqk, gv, gr, gg, nw, ones_bd, *, NL, NC):
    return _scan_call(_gla_kernel, "gla_scan", (gqk, gv, gr, gg), nw, ones_bd, 128, NL=NL, NC=NC)


def _gdn_kernel(q_ref, k_ref, v_ref, z_ref, g_ref, nw_ref, ones_ref, o_ref, od_ref, s_ref, *, NL, NC, G):
    C = CHUNK
    tri = _tri_pair()
    ri, ci = _cat_masks()
    m_incl = [ci <= ri, ci >= ri]
    m_strict = [ci < ri, ci > ri]
    dmask = ci == ri
    eye_cat = dmask.astype(F32)
    bd = (_iota((256, 256), 0) >> 6) == (_iota((256, 256), 1) >> 6)
    ones64 = jnp.ones((C, C), BF16)
    lane_r = _iota((128, 256), 0)
    head_c = _iota((128, 256), 1) >> 6
    eg = [(lane_r == head_c + 4 * d).astype(BF16) for d in range(2)]
    eb = [(lane_r == head_c + 8 + 4 * d).astype(BF16) for d in range(2)]

    def block_diag(y):
        return jnp.where(bd, jnp.concatenate([y.astype(BF16)] * 4, axis=0), jnp.zeros((), BF16))

    def mm_cat(x, y):
        return _dot(x.astype(BF16), block_diag(y))

    def pair_mask(ls):
        return jnp.logical_and((ri >> (ls + 1)) == (ci >> (ls + 1)), (ri >> ls) != (ci >> ls))

    def chunk(chains):
        n = range(len(chains))
        G_ = [c[0] for c in chains]
        D_ = [c[1] for c in chains]
        R_ = [pl.ds(c[2], C) for c in chains]
        last = [C - 1 if d == 0 else 0 for d in D_]
        q = [q_ref[G_[j], R_[j], :] for j in n]
        k = [k_ref[G_[j], R_[j], :] for j in n]
        v = [v_ref[G_[j], R_[j], :] for j in n]
        gt = [g_ref[G_[j], R_[j], :] for j in n]
        cum = [_sel_dot(tri[D_[j]], gt[j]) for j in n]
        cexp = [_dot_sel(cum[j], eg[D_[j]]) for j in n]
        ct = [_sel_dot(ones64, jnp.where(dmask, cexp[j], 0.0)) for j in n]
        decay = [jnp.exp(jnp.where(m_incl[D_[j]], cexp[j] - ct[j], -jnp.inf)) for j in n]
        bexp = [_dot_sel(gt[j], eb[D_[j]]) for j in n]
        kb = [k[j] * bexp[j] for j in n]
        kbd = [block_diag(k[j]) for j in n]
        a = [jnp.where(m_strict[D_[j]], _dot_nt(kb[j].astype(BF16), kbd[j]) * decay[j], 0.0) for j in n]
        t = [eye_cat - jnp.where(pair_mask(0), a[j], 0.0) for j in n]
        for ls in range(1, 6):
            ta = [mm_cat(t[j], jnp.where(pair_mask(ls), a[j], 0.0)) for j in n]
            t = [t[j] - mm_cat(ta[j], t[j]) for j in n]
        ecum = [jnp.exp(cexp[j]) for j in n]
        u = [mm_cat(t[j], v[j] * bexp[j]) for j in n]
        w = [mm_cat(t[j], kb[j] * ecum[j]) for j in n]
        s = [s_ref[G_[j], D_[j]] for j in n]
        sb = [s[j].astype(BF16) for j in n]
        v_new = [u[j] - _dot(w[j].astype(BF16), sb[j]) for j in n]
        attn = [jnp.where(m_incl[D_[j]], _dot_nt(q[j].astype(BF16), kbd[j]) * decay[j], 0.0) for j in n]
        o = [_dot((q[j] * ecum[j]).astype(BF16), sb[j]) + mm_cat(attn[j], v_new[j]) for j in n]
        cl = [cexp[j][last[j]:last[j] + 1] for j in n]
        kd = [(k[j] * jnp.exp(cl[j] - cexp[j])).astype(BF16) for j in n]
        upd = [_dot_tn(kd[j], v_new[j].astype(BF16)) for j in n]
        for j in n:
            od_ref[G_[j], D_[j], R_[j], :] = o[j]
            s_ref[G_[j], D_[j]] = s[j] * jnp.exp(cl[j]) + jnp.where(bd, upd[j], 0.0)

    def finish(gi, rows):
        o = od_ref[gi, 0, rows, :] + od_ref[gi, 1, rows, :]
        ms = _dot_sel(o * o, ones_ref[...]) * (1.0 / 64.0)
        o_ref[gi, rows, :] = o * lax.rsqrt(ms + RMS_EPS) * nw_ref[...] * _silu(z_ref[gi, rows, :])

    s_ref[...] = jnp.zeros(s_ref.shape, F32)
    _scan_loops(NL, NC, G, chunk, finish)


def _layer_norm(t, g, b):
    tc = t - jnp.mean(t, axis=-1, keepdims=True)
    var = jnp.mean(tc * tc, axis=-1, keepdims=True)
    return tc * lax.rsqrt(var + LN_EPS) * g + b


def _outproj_kernel(yal_ref, yac_ref, yb_ref, yc_ref, yd_ref, x_ref, g1_ref, scp_ref, sh_ref, w_ref, lg_ref, lb_ref,
                    wr_ref, br_ref, x1_ref, h_ref, ti_ref, tw_ref, *, alpha, n_lat_tiles):
    ya = jnp.where(pl.program_id(1) == n_lat_tiles, yac_ref[0], yal_ref[0])
    mix = (_dot(ya.astype(BF16), w_ref[0:GW]) + _dot(yb_ref[0].astype(BF16), w_ref[GW:2 * GW])
           + _dot(yc_ref[0].astype(BF16), w_ref[2 * GW:3 * GW]) + _dot(yd_ref[0].astype(BF16), w_ref[3 * GW:4 * GW]))
    x1 = _layer_norm(alpha * x_ref[0] + g1_ref[0] * mix, lg_ref[...], lb_ref[...])
    x1_ref[0] = x1
    h = x1 * scp_ref[0] + sh_ref[0]
    for s in range(8):
        h_ref[0, :, s, :] = h[:, s * 128:(s + 1) * 128]
    hh, hl = _split_bf16(h)
    wh, wl = _split_bf16(wr_ref[...])
    logits = _dot(hh, wh) + _dot(hl, wh) + _dot(hh, wl) + br_ref[...]
    lane = _iota(logits.shape, 1)
    vals, idxs = [], []
    for _ in range(TOP_K):
        m = jnp.max(logits, axis=-1, keepdims=True)
        idx = jnp.min(jnp.where(logits == m, lane, 128), axis=-1, keepdims=True)
        vals.append(m)
        idxs.append(idx)
        logits = jnp.where(lane == idx, -jnp.inf, logits)
    ex = [jnp.exp(v - vals[0]) for v in vals]
    inv = 1.0 / (ex[0] + ex[1] + ex[2] + ex[3])
    ti = jnp.zeros(lane.shape, I32)
    tw = jnp.zeros(lane.shape, F32)
    for kk in range(TOP_K):
        ti = jnp.where(lane == kk, idxs[kk], ti)
        tw = jnp.where(lane == kk, ex[kk] * inv, tw)
    ti_ref[0] = ti
    tw_ref[0] = tw


def _outproj(ya_lat, ya_ctx, yb, yc, yd, xcat, g1, scp, sh, w_bf16, lg, lb, wr, br, *, NL, NC, alpha):
    B, TT, _ = xcat.shape
    TR = NC
    n_lat_tiles = NL // TR
    mod_idx = lambda b, i: (jnp.where(i == n_lat_tiles, B, b), 0, 0)
    const2 = lambda b, i: (0, 0)
    row_blk = lambda c: pl.BlockSpec((1, TR, c), lambda b, i: (b, i, 0))
    mod = lambda: pl.BlockSpec((1, 1, D), mod_idx)
    return pl.pallas_call(
        functools.partial(_outproj_kernel, alpha=alpha, n_lat_tiles=n_lat_tiles),
        grid=(B, TT // TR),
        in_specs=[pl.BlockSpec((1, TR, GW), lambda b, i: (b, jnp.minimum(i, n_lat_tiles - 1), 0)),
                  pl.BlockSpec((1, TR, GW), lambda b, i: (b, 0, 0)),
                  row_blk(GW), row_blk(GW), row_blk(GW), row_blk(D), mod(), mod(), mod(),
                  pl.BlockSpec((D, D), const2), pl.BlockSpec((1, D), const2), pl.BlockSpec((1, D), const2),
                  pl.BlockSpec((D, 128), const2), pl.BlockSpec((1, 128), const2)],
        out_specs=[row_blk(D), pl.BlockSpec((1, TR, 8, 128), lambda b, i: (b, i, 0, 0)), row_blk(128), row_blk(128)],
        out_shape=[jax.ShapeDtypeStruct((B, TT, D), F32), jax.ShapeDtypeStruct((B, TT, 8, 128), F32),
                   jax.ShapeDtypeStruct((B, TT, 128), I32), jax.ShapeDtypeStruct((B, TT, 128), F32)],
        compiler_params=_cparams(("arbitrary", "arbitrary")),
        name="outproj_ln_router",
    )(ya_lat, ya_ctx, yb, yc, yd, xcat, g1, scp, sh, w_bf16, lg, lb, wr, br)


def _route(tidx, tw, tm):
    B, S, _ = tidx.shape
    e = tidx.reshape(B, S * TOP_K)
    order = jnp.argsort(e, axis=1, stable=True).astype(I32)
    tws = jnp.take_along_axis(tw.reshape(B, S * TOP_K), order, axis=1)
    counts = jnp.sum((e[:, :, None] == jnp.arange(N_EXP, dtype=I32)[None, None, :]).astype(I32), axis=1)
    starts = jnp.cumsum(counts, axis=1) - counts
    pad = ((0, 0), (0, tm))
    return (jnp.pad(order, pad)[:, None, :], jnp.pad(tws, pad)[:, None, :],
            counts.reshape(-1).astype(I32), starts.reshape(-1).astype(I32))


def _expert_kernel(cnt_ref, st_ref, ord_ref, tws_ref, h_ref, wg_ref, bg_ref, wu_ref, bu_ref, wd_ref, bd_ref,
                   acc_ref, xbuf, ybuf, *, tm):
    s = pl.program_id(0)
    e = pl.program_id(1)
    n = cnt_ref[s * N_EXP + e]
    st = st_ref[s * N_EXP + e]
    n_rows = h_ref.shape[0]

    @pl.when(e == 0)
    def _():
        acc_ref[...] = jnp.zeros(acc_ref.shape, F32)

    def tile(j, carry):
        base = st + j * tm
        m = n - j * tm

        def gather(r, c):
            xbuf[r] = h_ref[ord_ref[0, 0, base + r] >> 2]
            return c
        lax.fori_loop(0, tm, gather, 0, unroll=8)

        x = jnp.concatenate([xbuf[:, q, :] for q in range(8)], axis=1).astype(BF16)
        gate = jnp.minimum(_dot(x, wg_ref[0, 0]) + bg_ref[0, 0], SWIGLU_LIMIT)
        up = jnp.clip(_dot(x, wu_ref[0, 0]) + bu_ref[0, 0], -SWIGLU_LIMIT, SWIGLU_LIMIT)
        act = (up + 1.0) * gate * _sigmoid(SWIGLU_ALPHA * gate)
        y = _dot(act.astype(BF16), wd_ref[0, 0]) + bd_ref[0, 0]
        for q in range(8):
            ybuf[:, q, :] = y[:, q * 128:(q + 1) * 128]

        def scatter(grp, c):
            toks, ws = [], []
            for u in range(8):
                r = grp * 8 + u
                toks.append(jnp.where(r < m, ord_ref[0, 0, base + r] >> 2, n_rows))
                ws.append(jnp.where(r < m, tws_ref[0, 0, base + r], 0.0))
            old = [acc_ref[toks[u]] for u in range(8)]
            for u in range(8):
                acc_ref[toks[u]] = old[u] + ws[u] * ybuf[grp * 8 + u]
            return c
        lax.fori_loop(0, tm // 8, scatter, 0)
        return carry

    lax.fori_loop(0, (n + tm - 1) // tm, tile, 0)


def _experts(h4d, order, tws, counts, starts, wg, bg, wu, bu, wd, bd, *, tm, layer):
    B, S = h4d.shape[:2]
    n_ord = order.shape[-1]
    wspec = lambda: pl.BlockSpec((1, 1, D, D), lambda s, e, c, t: (layer, e, 0, 0))
    bspec = lambda: pl.BlockSpec((1, 1, 1, D), lambda s, e, c, t: (layer, e, 0, 0))
    smem = lambda: pl.BlockSpec((1, 1, n_ord), lambda s, e, c, t: (s, 0, 0), memory_space=pltpu.SMEM)
    tok = lambda rows, **kw: pl.BlockSpec((None, rows, 8, 128), lambda s, e, c, t: (s, 0, 0, 0), **kw)
    return pl.pallas_call(
        functools.partial(_expert_kernel, tm=tm),
        grid_spec=pltpu.PrefetchScalarGridSpec(
            num_scalar_prefetch=2,
            grid=(B, N_EXP),
            in_specs=[smem(), smem(), tok(S, pipeline_mode=pl.Buffered(1)),
                      wspec(), bspec(), wspec(), bspec(), wspec(), bspec()],
            out_specs=tok(S + 8),
            scratch_shapes=[pltpu.VMEM((tm, 8, 128), F32), pltpu.VMEM((tm, 8, 128), F32)]),
        out_shape=jax.ShapeDtypeStruct((B, S + 8, 8, 128), F32),
        compiler_params=_cparams(("arbitrary", "arbitrary")),
        name="moe_experts",
    )(counts, starts, order, tws, h4d, wg, bg, wu, bu, wd, bd)


def _combine_kernel(f_ref, x_ref, g2_ref, lg_ref, lb_ref, o_ref, *, alpha):
    f = jnp.concatenate([f_ref[0, :, q, :] for q in range(8)], axis=1)
    o_ref[0] = _layer_norm(alpha * x_ref[0] + g2_ref[0] * f, lg_ref[...], lb_ref[...])


def _combine(f4d, x1, g2, lg, lb, *, NL, NC, alpha):
    B, TT, _ = x1.shape
    TR = NC
    n_lat_tiles = NL // TR
    mod_idx = lambda b, i: (jnp.where(i == n_lat_tiles, B, b), 0, 0)
    return pl.pallas_call(
        functools.partial(_combine_kernel, alpha=alpha),
        grid=(B, TT // TR),
        in_specs=[pl.BlockSpec((1, TR, 8, 128), lambda b, i: (b, i, 0, 0)),
                  pl.BlockSpec((1, TR, D), lambda b, i: (b, i, 0)), pl.BlockSpec((1, 1, D), mod_idx),
                  pl.BlockSpec((1, D), lambda b, i: (0, 0)), pl.BlockSpec((1, D), lambda b, i: (0, 0))],
        out_specs=pl.BlockSpec((1, TR, D), lambda b, i: (b, i, 0)),
        out_shape=jax.ShapeDtypeStruct((B, TT, D), F32),
        compiler_params=_cparams(("arbitrary", "arbitrary")),
        name="moe_combine_ln",
    )(f4d, x1, g2, lg, lb)


def _gdn_scan(dq, dk, dv, dz, dg, nw, ones_bd, *, NL, NC):
    return _scan_call(_gdn_kernel, "gdn_scan", (dq, dk, dv, dz, dg), nw, ones_bd, 256, NL=NL, NC=NC)


def _expert_tile_rows(tokens_per_sample):
    mean = tokens_per_sample * TOP_K / N_EXP
    return max(32, int(-(-(mean * 10 / 9) // 32) * 32))


def kernel(x, c, ctx, c_ctx, w_ada, b_ada, w_in, hy_conv_w, hy_conv_b, hy_w1, hy_b1, hy_w2, hy_b2, hy_w3, hy_b3, hy_w4, hy_freq, hy_d, sc_conv_w, gla_w_a, gla_b_a, gla_norm_w, gdn_conv_w, gdn_a_log, gdn_dt_bias, gdn_norm_w, w_out, ln1_g, ln1_b, w_router, b_router, w_gate, b_gate, w_up, b_up, w_down, b_down, ln2_g, ln2_b):
    B, NL, _ = x.shape
    NC = ctx.shape[1]
    TT = NL + NC
    T = B * TT
    depth = w_ada.shape[0]
    alpha = (2 * depth) ** 0.25
    assert NL % NC == 0 and NC % CHUNK == 0 and (NC & (NC - 1)) == 0

    xcat = jnp.concatenate([x, ctx], axis=1)
    n_mod = -(-(B + 1) // 8) * 8
    cc = jnp.concatenate([c, c_ctx[None], jnp.zeros((n_mod - B - 1, D), F32)], axis=0)
    mods = _ada_mods(cc, w_ada, b_ada)

    pp = _prep_inproj_params(w_in, hy_conv_w, hy_conv_b, hy_d, sc_conv_w, gla_w_a, gla_b_a, gdn_conv_w,
                             gdn_a_log, gdn_dt_bias)
    ones_bd = _block_ones(256, 64)
    hy_args = (jnp.pad(hy_w1, ((0, 0), (0, HY_HID - HY_EMB), (0, 0))), hy_b1[:, None], hy_w2, hy_b2[:, None],
               hy_w3, hy_b3[:, None], hy_w4, hy_freq)
    tabs_l = _dft_tables(NL)
    tabs_c = _dft_tables(NC)
    spec_l = _hyena_spectrum(NL, tabs_l[0], tabs_l[1], *hy_args)
    spec_c = _hyena_spectrum(NC, tabs_c[0], tabs_c[1], *hy_args)
    tabs_l = tuple(t.astype(BF16) for t in tabs_l)
    tabs_c = tuple(t.astype(BF16) for t in tabs_c)

    w_out_b = w_out.astype(BF16)
    wr = _pad_cols(w_router, 128)
    br = jnp.pad(b_router, ((0, 0), (0, 128 - N_EXP)), constant_values=NEG_BIG)[:, None, :]
    wg, wu, wd = w_gate.astype(BF16), w_up.astype(BF16), w_down.astype(BF16)
    tm = _expert_tile_rows(TT)

    for l in range(depth):
        sh1, sc1, g1, sh2, sc2, g2 = (mods[l, :, k * D:(k + 1) * D][:, None, :] for k in range(6))
        zb, e, x0, ysc, gqk, gv, gr, gg, dq, dk, dv, dz, dg = _inproj(
            xcat, 1.0 + sc1, sh1, pp["w"][l], pp["hyw"][l], pp["hyb"][l], pp["hyd"][l], pp["scw"][l],
            pp["glaw"][l], pp["glab"][l], pp["gdnw"][l], pp["gdna"][l], pp["gdnd"][l], ones_bd, NL=NL, NC=NC)
        ya_lat = _hyconv(zb, e, x0, spec_l[l:l + 1], tabs_l, L=NL, row_blk=0)
        ya_ctx = _hyconv(zb, e, x0, spec_c[l:l + 1], tabs_c, L=NC, row_blk=NL // NC)
        yc = _gla_scan(gqk, gv, gr, gg, jnp.tile(gla_norm_w[l], 4)[None], ones_bd, NL=NL, NC=NC)
        yd = _gdn_scan(dq, dk, dv, dz, dg, jnp.tile(gdn_norm_w[l], 4)[None], ones_bd, NL=NL, NC=NC)
        x1, h, ti, tw = _outproj(ya_lat, ya_ctx, ysc, yc, yd, xcat, g1, 1.0 + sc2, sh2, w_out_b[l], ln1_g[l][None],
                                 ln1_b[l][None], wr[l], br[l], NL=NL, NC=NC, alpha=alpha)
        n_routed = NL if l == depth - 1 else TT
        order, tws, counts, starts = _route(ti[:, :n_routed, :TOP_K], tw[:, :n_routed, :TOP_K], tm)
        f = _experts(h, order, tws, counts, starts, wg, b_gate[:, :, None, :], wu, b_up[:, :, None, :], wd,
                     b_down[:, :, None, :], tm=tm, layer=l)
        xcat = _combine(f, x1, g2, ln2_g[l][None], ln2_b[l][None], NL=NL, NC=NC, alpha=alpha)
    return xcat[:, :NL]
```

```python
import functools
import math

import numpy as np
import jax
import jax.numpy as jnp
from jax import lax
from jax.experimental import pallas as pl
from jax.experimental.pallas import tpu as pltpu

F32 = jnp.float32
BF16 = jnp.bfloat16
I32 = jnp.int32
HI = lax.Precision.HIGHEST

D = 1024
GW = 256
CHUNK = 64
ROW_W = 64
GLA_DK = 32
GLA_TAU = 16.0
GDN_DK = 64
N_EXP = 32
TOP_K = 4
SWIGLU_LIMIT = 7.0
SWIGLU_ALPHA = 1.702
LN_EPS = 1e-5
RMS_EPS = 1e-6
L2_EPS = 1e-6
HY_EMB = 33
HY_HID = 64
NEG_BIG = -1e30
P_HY, P_SC, P_GLA, P_GDN = 768, 768, 896, 1152
P_IN = P_HY + P_SC + P_GLA + P_GDN
VMEM_LIMIT = 56 * 1024 * 1024


def _cparams(sem):
    return pltpu.CompilerParams(dimension_semantics=sem, vmem_limit_bytes=VMEM_LIMIT)


def _split_bf16(x):
    hi = x.astype(BF16)
    lo = (x - hi.astype(F32)).astype(BF16)
    return hi, lo


def _dot(a, b):
    return jnp.dot(a, b, preferred_element_type=F32)


def _dot_nt(a, b):
    return lax.dot_general(a, b, (((1,), (1,)), ((), ())), preferred_element_type=F32)


def _dot_tn(a, b):
    return lax.dot_general(a, b, (((0,), (0,)), ((), ())), preferred_element_type=F32)


def _dot_sel(x, sel_bf16):
    h1 = x.astype(BF16)
    r1 = x - h1.astype(F32)
    h2 = r1.astype(BF16)
    h3 = (r1 - h2.astype(F32)).astype(BF16)
    return _dot(h1, sel_bf16) + _dot(h2, sel_bf16) + _dot(h3, sel_bf16)


def _sel_dot(sel_bf16, x):
    h1 = x.astype(BF16)
    r1 = x - h1.astype(F32)
    h2 = r1.astype(BF16)
    h3 = (r1 - h2.astype(F32)).astype(BF16)
    return _dot(sel_bf16, h1) + _dot(sel_bf16, h2) + _dot(sel_bf16, h3)


def _iota(shape, dim):
    return lax.broadcasted_iota(I32, shape, dim)


def _sigmoid(x):
    return 1.0 / (1.0 + jnp.exp(-x))


def _silu(x):
    return x * _sigmoid(x)


def _softplus(x):
    return jnp.maximum(x, 0.0) + jnp.log(1.0 + jnp.exp(-jnp.abs(x)))


def _log_sigmoid(x):
    return -_softplus(-x)


def _ada_kernel(cc_ref, w_ref, b_ref, o_ref):
    o_ref[0] = jnp.dot(_silu(cc_ref[...]), w_ref[0], precision=HI, preferred_element_type=F32) + b_ref[0]


def _ada_mods(cc, w_ada, b_ada):
    depth, _, n6 = w_ada.shape
    rows = cc.shape[0]
    tn = 1536
    return pl.pallas_call(
        _ada_kernel,
        grid=(depth, n6 // tn),
        in_specs=[pl.BlockSpec((rows, D), lambda l, j: (0, 0)),
                  pl.BlockSpec((1, D, tn), lambda l, j: (l, 0, j)),
                  pl.BlockSpec((1, 1, tn), lambda l, j: (l, 0, j))],
        out_specs=pl.BlockSpec((1, rows, tn), lambda l, j: (l, 0, j)),
        out_shape=jax.ShapeDtypeStruct((depth, rows, n6), F32),
        compiler_params=_cparams(("arbitrary", "arbitrary")),
        name="ada_mods",
    )(cc, w_ada, b_ada.reshape(depth, 1, n6))


def _conv3(x, w, rowlen):
    n = x.shape[0]
    pos = _iota(x.shape, 0) & (rowlen - 1)
    xm = jnp.where(pos == 0, 0.0, pltpu.roll(x, 1, axis=0))
    xp = jnp.where(pos == rowlen - 1, 0.0, pltpu.roll(x, n - 1, axis=0))
    return xm * w[0:1] + x * w[1:2] + xp * w[2:3]


def _inproj_kernel(x_ref, scp_ref, sh_ref, w_ref, hyw_ref, hyb_ref, hyd_ref, scw_ref, glaw_ref, glab_ref,
                   gdnw_ref, gdna_ref, gdnd_ref, ones_ref,
                   zb_ref, e_ref, x0_ref, ysc_ref, gqk_ref, gv_ref, gr_ref, gg_ref,
                   dq_ref, dk_ref, dv_ref, dz_ref, dg_ref, *, n_lat_tiles, ctx_len):
    i = pl.program_id(1)
    rowlen = jnp.where(i == n_lat_tiles, ctx_len, ROW_W)
    xm = (x_ref[0] * scp_ref[0] + sh_ref[0]).astype(BF16)

    u = _conv3(_dot(xm, w_ref[:, 0:P_HY]), hyw_ref[...], rowlen) + hyb_ref[...]
    z = u[:, GW:2 * GW] * u[:, 2 * GW:3 * GW]
    zb_ref[0] = z.astype(BF16)
    e_ref[0] = z * hyd_ref[...]
    x0_ref[0] = u[:, 0:GW]

    ps = _dot(xm, w_ref[:, P_HY:P_HY + P_SC])
    ysc_ref[0] = ps[:, 0:GW] * _conv3(ps[:, GW:2 * GW] * ps[:, 2 * GW:3 * GW], scw_ref[...], rowlen)

    pg = _dot(xm, w_ref[:, P_HY + P_SC:P_HY + P_SC + P_GLA])
    gqk_ref[0] = pg[:, 0:256]
    gv_ref[0] = pg[:, 256:512]
    gr_ref[0] = pg[:, 512:768]
    ah, al = _split_bf16(pg[:, 768:896])
    wh, wl = _split_bf16(glaw_ref[...])
    logit = _dot(ah, wh) + _dot(al, wh) + _dot(ah, wl) + glab_ref[...]
    gg_ref[0] = _log_sigmoid(logit) * (1.0 / GLA_TAU)

    o3 = P_HY + P_SC + P_GLA
    pd = _dot(xm, w_ref[:, o3:o3 + P_GDN])
    qkv = _silu(_conv3(pd[:, 0:768], gdnw_ref[...], rowlen))
    ones_bd = ones_ref[...]
    q = qkv[:, 0:256]
    k = qkv[:, 256:512]
    dq_ref[0] = q * lax.rsqrt(_dot_sel(q * q, ones_bd) + L2_EPS) * (GDN_DK ** -0.5)
    dk_ref[0] = k * lax.rsqrt(_dot_sel(k * k, ones_bd) + L2_EPS)
    dv_ref[0] = qkv[:, 512:768]
    dz_ref[0] = pd[:, 768:1024]
    gl = pd[:, 1024:1152]
    lane = _iota(gl.shape, 1)
    dg_ref[0] = jnp.where(lane < 8, gdna_ref[...] * _softplus(gl + gdnd_ref[...]), _sigmoid(gl))


def _inproj(xcat, scp, sh, w_bf16, hyw, hyb, hyd, scw, glaw, glab, gdnw, gdna, gdnd, ones_bd, *, NL, NC):
    B, TT, _ = xcat.shape
    TR = NC
    n_lat_tiles = NL // TR
    nt = TT // TR
    mod_idx = lambda b, i: (jnp.where(i == n_lat_tiles, B, b), 0, 0)
    const2 = lambda b, i: (0, 0)
    row_blk = lambda c: pl.BlockSpec((1, TR, c), lambda b, i: (b, i, 0))
    outs = [(GW, BF16)] + [(GW, F32)] * 11 + [(128, F32)]
    return pl.pallas_call(
        functools.partial(_inproj_kernel, n_lat_tiles=n_lat_tiles, ctx_len=NC),
        grid=(B, nt),
        in_specs=[row_blk(D),
                  pl.BlockSpec((1, 1, D), mod_idx), pl.BlockSpec((1, 1, D), mod_idx),
                  pl.BlockSpec((D, P_IN), const2),
                  pl.BlockSpec((3, P_HY), const2), pl.BlockSpec((1, P_HY), const2), pl.BlockSpec((1, GW), const2),
                  pl.BlockSpec((3, GW), const2),
                  pl.BlockSpec((128, 256), const2), pl.BlockSpec((1, 256), const2),
                  pl.BlockSpec((3, 768), const2), pl.BlockSpec((1, 128), const2), pl.BlockSpec((1, 128), const2),
                  pl.BlockSpec((256, 256), const2)],
        out_specs=[row_blk(c) for c, _ in outs],
        out_shape=[jax.ShapeDtypeStruct((B, TT, c), dt) for c, dt in outs],
        compiler_params=_cparams(("arbitrary", "arbitrary")),
        name="inproj_local",
    )(xcat, scp, sh, w_bf16, hyw, hyb, hyd, scw, glaw, glab, gdnw, gdna, gdnd, ones_bd)


def _dft_tables(L):
    N = 2 * L
    f = jnp.arange(L, dtype=I32)[:, None]
    t = jnp.arange(L, dtype=I32)[None, :]
    ang = ((f * t) % N).astype(F32) * (2.0 * math.pi / N)
    cos = jnp.cos(ang)
    sin = jnp.sin(ang)
    nyq = jnp.where(t % 2 == 0, 1.0, -1.0).astype(F32)
    mc = cos
    ms = jnp.where(f == 0, nyq, -sin)
    wf = jnp.where(f == 0, 1.0, 2.0).astype(F32) / N
    ic = (cos * wf).T
    isn = jnp.where(f == 0, nyq / N, -sin * wf).T
    return mc, ms, ic, isn


def _hyfilt_kernel(zf_ref, win_ref, w1_ref, b1_ref, w2_ref, b2_ref, w3_ref, b3_ref, w4_ref, fr_ref,
                   mc_ref, ms_ref, o_ref, ks_ref):
    j = pl.program_id(1)

    @pl.when(j == 0)
    def _():
        dot = lambda a, b: jnp.dot(a, b, precision=HI, preferred_element_type=F32)
        h = jnp.sin(fr_ref[0, 0:1] * (dot(zf_ref[...], w1_ref[0]) + b1_ref[0]))
        h = jnp.sin(fr_ref[0, 1:2] * (dot(h, w2_ref[0]) + b2_ref[0]))
        h = jnp.sin(fr_ref[0, 2:3] * (dot(h, w3_ref[0]) + b3_ref[0]))
        k = dot(h, w4_ref[0])
        win = win_ref[...]
        kf = k[:, 0:GW] * win
        kb = k[:, GW:2 * GW] * win
        kb = jnp.where(_iota(kb.shape, 0) == 0, 0.0, kb)
        l1 = jnp.sum(jnp.abs(kf), axis=0, keepdims=True) + jnp.sum(jnp.abs(kb), axis=0, keepdims=True)
        kf = kf / l1
        kb = kb / l1
        ks_ref[:, 0:GW] = kf + kb
        ks_ref[:, GW:2 * GW] = kf - kb

    ks = ks_ref[...]
    o_ref[0, 0] = jnp.dot(mc_ref[...], ks, precision=HI, preferred_element_type=F32)
    o_ref[0, 1] = jnp.dot(ms_ref[...], ks, precision=HI, preferred_element_type=F32)


def _hyena_spectrum(L, mc, ms, w1p, b1, w2, b2, w3, b3, w4, freq):
    depth = w1p.shape[0]
    t = jnp.linspace(0.0, 1.0, L, dtype=F32)[:, None]
    bands = (HY_EMB - 1) // 2
    ang = 2.0 * math.pi * jnp.arange(L, dtype=F32)[:, None] / L
    f = jnp.linspace(1e-4, bands - 1, bands, dtype=F32)[None, :]
    zf = jnp.concatenate([t, jnp.cos(f * ang), -jnp.sin(f * ang), jnp.zeros((L, HY_HID - HY_EMB), F32)], axis=-1)
    max_decay = math.log(1e-2) / 0.3
    min_decay = math.log(1e-2) / 1.5
    deltas = jnp.abs(jnp.linspace(min_decay, max_decay, GW, dtype=F32))
    win = jnp.exp(-t * deltas[None, :])
    ft = min(L, 512)
    c2 = lambda l, j: (0, 0)
    lw = lambda *s: pl.BlockSpec((1,) + s, lambda l, j: (l,) + (0,) * len(s))
    a = pl.pallas_call(
        _hyfilt_kernel,
        grid=(depth, L // ft),
        in_specs=[pl.BlockSpec((L, HY_HID), c2), pl.BlockSpec((L, GW), c2),
                  lw(HY_HID, HY_HID), lw(1, HY_HID), lw(HY_HID, HY_HID), lw(1, HY_HID),
                  lw(HY_HID, HY_HID), lw(1, HY_HID), lw(HY_HID, 2 * GW), lw(3, HY_HID),
                  pl.BlockSpec((ft, L), lambda l, j: (j, 0)), pl.BlockSpec((ft, L), lambda l, j: (j, 0))],
        out_specs=pl.BlockSpec((1, 2, ft, 2 * GW), lambda l, j: (l, 0, j, 0)),
        out_shape=jax.ShapeDtypeStruct((depth, 2, L, 2 * GW), F32),
        scratch_shapes=[pltpu.VMEM((L, 2 * GW), F32)],
        compiler_params=_cparams(("arbitrary", "arbitrary")),
        name="hyena_filter_spectrum",
    )(zf, win, w1p, b1, w2, b2, w3, b3, w4, freq, mc, ms)
    k_re = a[:, 0, :, 0:GW]
    k_im = a[:, 1, :, GW:2 * GW]
    k_im = k_im.at[:, 0, :].set(a[:, 1, 0, 0:GW])
    return jnp.stack([k_re, k_im], axis=1)


def _hyconv_kernel(zb_ref, e_ref, x0_ref, k_ref, mc_ref, ms_ref, ic_ref, is_ref, o_ref, acc_ref, *, G):
    j = pl.program_id(1)
    nj = pl.num_programs(1)
    kt = k_ref[0, 0]
    kb = k_ref[0, 1]
    row0 = jnp.logical_and(_iota(kt.shape, 0) == 0, j == 0)
    mc = mc_ref[...]
    ms = ms_ref[...]
    ic = ic_ref[...]
    isn = is_ref[...]
    for g in range(G):
        z = zb_ref[g]
        zt = _dot(mc, z)
        zi = _dot(ms, z)
        yt = zt * kt - jnp.where(row0, 0.0, zi * kb)
        yi = jnp.where(row0, zi * kb, zt * kb + zi * kt)
        contrib = _dot(ic, yt.astype(BF16)) + _dot(isn, yi.astype(BF16))

        @pl.when(j == 0)
        def _():
            acc_ref[g] = contrib

        @pl.when(j > 0)
        def _():
            acc_ref[g] += contrib

    @pl.when(j == nj - 1)
    def _():
        for g in range(G):
            o_ref[g] = (acc_ref[g] + e_ref[g]) * x0_ref[g]


def _hyconv(zb, e, x0, kspec_l, tabs, *, L, row_blk):
    B = zb.shape[0]
    G = 2 if B % 2 == 0 else 1
    ft = min(L, 256)
    mc, ms, ic, isn = tabs
    seg = lambda: pl.BlockSpec((G, L, GW), lambda b, j: (b, row_blk, 0))
    return pl.pallas_call(
        functools.partial(_hyconv_kernel, G=G),
        grid=(B // G, L // ft),
        in_specs=[seg(), seg(), seg(),
                  pl.BlockSpec((1, 2, ft, GW), lambda b, j: (0, 0, j, 0)),
                  pl.BlockSpec((ft, L), lambda b, j: (j, 0)), pl.BlockSpec((ft, L), lambda b, j: (j, 0)),
                  pl.BlockSpec((L, ft), lambda b, j: (0, j)), pl.BlockSpec((L, ft), lambda b, j: (0, j))],
        out_specs=pl.BlockSpec((G, L, GW), lambda b, j: (b, 0, 0)),
        out_shape=jax.ShapeDtypeStruct((B, L, GW), F32),
        scratch_shapes=[pltpu.VMEM((G, L, GW), F32)],
        compiler_params=_cparams(("arbitrary", "arbitrary")),
        name=f"hyena_longconv_{L}",
    )(zb, e, x0, kspec_l, mc, ms, ic, isn)


def _pad_cols(a, n):
    return jnp.pad(a, [(0, 0)] * (a.ndim - 1) + [(0, n - a.shape[-1])])


def _prep_inproj_params(w_in, hy_conv_w, hy_conv_b, hy_d, sc_conv_w, gla_w_a, gla_b_a, gdn_conv_w, gdn_a_log,
                        gdn_dt_bias):
    depth = w_in.shape[0]
    o1, o2, o3 = 768, 1536, 1536 + 800
    w = jnp.concatenate([w_in[..., 0:o2], _pad_cols(w_in[..., o2:o3], P_GLA), _pad_cols(w_in[..., o3:], P_GDN)],
                        axis=-1).astype(BF16)
    glaw = jnp.zeros((depth, 128, 256), F32)
    glaw = glaw.at[:, 0:16, 0:128].set(gla_w_a[:, 0]).at[:, 16:32, 128:256].set(gla_w_a[:, 1])
    glab = gla_b_a.reshape(depth, 1, 256)
    gdna = _pad_cols(-jnp.exp(gdn_a_log.astype(F32)).reshape(depth, 1, 8), 128)
    gdnd = _pad_cols(gdn_dt_bias.astype(F32).reshape(depth, 1, 8), 128)
    return dict(w=w, hyw=hy_conv_w, hyb=hy_conv_b[:, None, :], hyd=hy_d[:, None, :], scw=sc_conv_w, glaw=glaw,
                glab=glab, gdnw=gdn_conv_w, gdna=gdna, gdnd=gdnd)


def _block_ones(n, blk):
    r = np.arange(n)
    return jnp.asarray((r[:, None] // blk) == (r[None, :] // blk), BF16)


def _tri_pair():
    r = _iota((CHUNK, CHUNK), 0)
    c = _iota((CHUNK, CHUNK), 1)
    return [(r >= c).astype(BF16), (r <= c).astype(BF16)]


def _cat_masks():
    r = _iota((CHUNK, 4 * CHUNK), 0)
    c = _iota((CHUNK, 4 * CHUNK), 1) & (CHUNK - 1)
    return r, c


SCAN_G = 2


def _scan_loops(NL, NC, G, chunk_fn, finish_fn):
    def phase(n, base0):
        def body(i, carry):
            bf = pl.multiple_of(base0 + i * CHUNK, CHUNK)
            bb = pl.multiple_of(base0 + (n - 1 - i) * CHUNK, CHUNK)
            chunk_fn([(g, d, (bf, bb)[d]) for g in range(G) for d in range(2)])
            return carry
        lax.fori_loop(0, n, body, 0)

    phase(NC // CHUNK, NL)
    phase(NL // CHUNK, 0)

    def fin(i, carry):
        base = pl.multiple_of(i * NC, NC)
        for g in range(G):
            finish_fn(g, pl.ds(base, NC))
        return carry
    lax.fori_loop(0, (NL + NC) // NC, fin, 0)


def _gla_kernel(qk_ref, v_ref, r_ref, g_ref, nw_ref, ones_ref, o_ref, od_ref, st_ref, *, NL, NC, G):
    C = CHUNK
    scale = GLA_DK ** -0.5
    tri = _tri_pair()
    ri, ci = _cat_masks()
    cmask = [ci <= ri, ci >= ri]
    hm = (_iota((256, 128), 0) >> 6) == (_iota((256, 128), 1) >> 5)
    vbd = (_iota((256, 256), 0) >> 6) == (_iota((256, 256), 1) >> 6)

    def chunk(chains):
        n = range(len(chains))
        G_ = [c[0] for c in chains]
        D_ = [c[1] for c in chains]
        R_ = [pl.ds(c[2], C) for c in chains]
        mid = [C // 2 if d == 0 else C // 2 - 1 for d in D_]
        last = [C - 1 if d == 0 else 0 for d in D_]
        q = [qk_ref[G_[j], R_[j], 0:128] * scale for j in n]
        k = [qk_ref[G_[j], R_[j], 128:256] for j in n]
        v = [v_ref[G_[j], R_[j], :].astype(BF16) for j in n]
        g = [g_ref[G_[j], R_[j], D_[j] * 128:(D_[j] + 1) * 128] for j in n]
        b = [_sel_dot(tri[D_[j]], g[j]) for j in n]
        b_mid = [b[j][mid[j]:mid[j] + 1] for j in n]
        b_last = [b[j][last[j]:last[j] + 1] for j in n]
        qe = [(q[j] * jnp.exp(b[j] - b_mid[j])).astype(BF16) for j in n]
        ke = [(k[j] * jnp.exp(b_mid[j] - b[j])).astype(BF16) for j in n]
        kbd = [jnp.where(hm, jnp.concatenate([ke[j]] * 4, axis=0), jnp.zeros((), BF16)) for j in n]
        a = [jnp.where(cmask[D_[j]], _dot_nt(qe[j], kbd[j]), 0.0).astype(BF16) for j in n]
        vb = [jnp.where(vbd, jnp.concatenate([v[j]] * 4, axis=0), jnp.zeros((), BF16)) for j in n]
        st = [st_ref[G_[j], D_[j]] for j in n]
        qb = [(q[j] * jnp.exp(b[j])).astype(BF16) for j in n]
        o = [_dot(a[j], vb[j]) + _dot_nt(qb[j], st[j].astype(BF16)) for j in n]
        kd = [(k[j] * jnp.exp(b_last[j] - b[j])).astype(BF16) for j in n]
        upd = [_dot_tn(v[j], kd[j]) for j in n]
        for j in n:
            od_ref[G_[j], D_[j], R_[j], :] = o[j]
            st_ref[G_[j], D_[j]] = st[j] * jnp.exp(b_last[j]) + jnp.where(hm, upd[j], 0.0)

    def finish(gi, rows):
        o = od_ref[gi, 0, rows, :] + od_ref[gi, 1, rows, :]
        ms = _dot_sel(o * o, ones_ref[...]) * (1.0 / 64.0)
        o_ref[gi, rows, :] = o * lax.rsqrt(ms + RMS_EPS) * nw_ref[...] * _silu(r_ref[gi, rows, :])

    st_ref[...] = jnp.zeros(st_ref.shape, F32)
    _scan_loops(NL, NC, G, chunk, finish)


def _scan_call(kernel_fn, name, arrays, nw, ones_bd, state_cols, *, NL, NC):
    B, TT, _ = arrays[0].shape
    G = SCAN_G if B % SCAN_G == 0 else 1
    blk = lambda c: pl.BlockSpec((G, TT, c), lambda b: (b, 0, 0), pipeline_mode=pl.Buffered(1))
    return pl.pallas_call(
        functools.partial(kernel_fn, NL=NL, NC=NC, G=G),
        grid=(B // G,),
        in_specs=[blk(a.shape[-1]) for a in arrays] + [pl.BlockSpec((1, 256), lambda b: (0, 0)),
                                                        pl.BlockSpec((256, 256), lambda b: (0, 0))],
        out_specs=pl.BlockSpec((G, TT, 256), lambda b: (b, 0, 0)),
        out_shape=jax.ShapeDtypeStruct((B, TT, 256), F32),
        scratch_shapes=[pltpu.VMEM((G, 2, TT, 256), F32), pltpu.VMEM((G, 2, 256, state_cols), F32)],
        compiler_params=_cparams(("arbitrary",)),
        name=name,
    )(*arrays, nw, ones_bd)


def _gla_scan(gqk, gv, gr, gg, nw, ones_bd, *, NL, NC):
    return _scan_call(_gla_kernel, "gla_scan", (gqk, gv, gr, gg), nw, ones_bd, 128, NL=NL, NC=NC)


def _gdn_kernel(q_ref, k_ref, v_ref, z_ref, g_ref, nw_ref, ones_ref, o_ref, od_ref, s_ref, *, NL, NC, G):
    C = CHUNK
    tri = _tri_pair()
    ri, ci = _cat_masks()
    m_incl = [ci <= ri, ci >= ri]
    m_strict = [ci < ri, ci > ri]
    dmask = ci == ri
    eye_cat = dmask.astype(F32)
    bd = (_iota((256, 256), 0) >> 6) == (_iota((256, 256), 1) >> 6)
    ones64 = jnp.ones((C, C), BF16)
    lane_r = _iota((128, 256), 0)
    head_c = _iota((128, 256), 1) >> 6
    eg = [(lane_r == head_c + 4 * d).astype(BF16) for d in range(2)]
    eb = [(lane_r == head_c + 8 + 4 * d).astype(BF16) for d in range(2)]

    def block_diag(y):
        return jnp.where(bd, jnp.concatenate([y.astype(BF16)] * 4, axis=0), jnp.zeros((), BF16))

    def mm_cat(x, y):
        return _dot(x.astype(BF16), block_diag(y))

    def pair_mask(ls):
        return jnp.logical_and((ri >> (ls + 1)) == (ci >> (ls + 1)), (ri >> ls) != (ci >> ls))

    def chunk(chains):
        n = range(len(chains))
        G_ = [c[0] for c in chains]
        D_ = [c[1] for c in chains]
        R_ = [pl.ds(c[2], C) for c in chains]
        last = [C - 1 if d == 0 else 0 for d in D_]
        q = [q_ref[G_[j], R_[j], :] for j in n]
        k = [k_ref[G_[j], R_[j], :] for j in n]
        v = [v_ref[G_[j], R_[j], :] for j in n]
        gt = [g_ref[G_[j], R_[j], :] for j in n]
        cum = [_sel_dot(tri[D_[j]], gt[j]) for j in n]
        cexp = [_dot_sel(cum[j], eg[D_[j]]) for j in n]
        ct = [_sel_dot(ones64, jnp.where(dmask, cexp[j], 0.0)) for j in n]
        decay = [jnp.exp(jnp.where(m_incl[D_[j]], cexp[j] - ct[j], -jnp.inf)) for j in n]
        bexp = [_dot_sel(gt[j], eb[D_[j]]) for j in n]
        kb = [k[j] * bexp[j] for j in n]
        kbd = [block_diag(k[j]) for j in n]
        a = [jnp.where(m_strict[D_[j]], _dot_nt(kb[j].astype(BF16), kbd[j]) * decay[j], 0.0) for j in n]
        t = [eye_cat - jnp.where(pair_mask(0), a[j], 0.0) for j in n]
        for ls in range(1, 6):
            ta = [mm_cat(t[j], jnp.where(pair_mask(ls), a[j], 0.0)) for j in n]
            t = [t[j] - mm_cat(ta[j], t[j]) for j in n]
        ecum = [jnp.exp(cexp[j]) for j in n]
        u = [mm_cat(t[j], v[j] * bexp[j]) for j in n]
        w = [mm_cat(t[j], kb[j] * ecum[j]) for j in n]
        s = [s_ref[G_[j], D_[j]] for j in n]
        sb = [s[j].astype(BF16) for j in n]
        v_new = [u[j] - _dot(w[j].astype(BF16), sb[j]) for j in n]
        attn = [jnp.where(m_incl[D_[j]], _dot_nt(q[j].astype(BF16), kbd[j]) * decay[j], 0.0) for j in n]
        o = [_dot((q[j] * ecum[j]).astype(BF16), sb[j]) + mm_cat(attn[j], v_new[j]) for j in n]
        cl = [cexp[j][last[j]:last[j] + 1] for j in n]
        kd = [(k[j] * jnp.exp(cl[j] - cexp[j])).astype(BF16) for j in n]
        upd = [_dot_tn(kd[j], v_new[j].astype(BF16)) for j in n]
        for j in n:
            od_ref[G_[j], D_[j], R_[j], :] = o[j]
            s_ref[G_[j], D_[j]] = s[j] * jnp.exp(cl[j]) + jnp.where(bd, upd[j], 0.0)

    def finish(gi, rows):
        o = od_ref[gi, 0, rows, :] + od_ref[gi, 1, rows, :]
        ms = _dot_sel(o * o, ones_ref[...]) * (1.0 / 64.0)
        o_ref[gi, rows, :] = o * lax.rsqrt(ms + RMS_EPS) * nw_ref[...] * _silu(z_ref[gi, rows, :])

    s_ref[...] = jnp.zeros(s_ref.shape, F32)
    _scan_loops(NL, NC, G, chunk, finish)


def _layer_norm(t, g, b):
    tc = t - jnp.mean(t, axis=-1, keepdims=True)
    var = jnp.mean(tc * tc, axis=-1, keepdims=True)
    return tc * lax.rsqrt(var + LN_EPS) * g + b


def _outproj_kernel(yal_ref, yac_ref, yb_ref, yc_ref, yd_ref, x_ref, g1_ref, scp_ref, sh_ref, w_ref, lg_ref, lb_ref,
                    wr_ref, br_ref, x1_ref, h_ref, ti_ref, tw_ref, *, alpha, n_lat_tiles):
    ya = jnp.where(pl.program_id(1) == n_lat_tiles, yac_ref[0], yal_ref[0])
    mix = (_dot(ya.astype(BF16), w_ref[0:GW]) + _dot(yb_ref[0].astype(BF16), w_ref[GW:2 * GW])
           + _dot(yc_ref[0].astype(BF16), w_ref[2 * GW:3 * GW]) + _dot(yd_ref[0].astype(BF16), w_ref[3 * GW:4 * GW]))
    x1 = _layer_norm(alpha * x_ref[0] + g1_ref[0] * mix, lg_ref[...], lb_ref[...])
    x1_ref[0] = x1
    h = x1 * scp_ref[0] + sh_ref[0]
    for s in range(8):
        h_ref[0, :, s, :] = h[:, s * 128:(s + 1) * 128]
    hh, hl = _split_bf16(h)
    wh, wl = _split_bf16(wr_ref[...])
    logits = _dot(hh, wh) + _dot(hl, wh) + _dot(hh, wl) + br_ref[...]
    lane = _iota(logits.shape, 1)
    vals, idxs = [], []
    for _ in range(TOP_K):
        m = jnp.max(logits, axis=-1, keepdims=True)
        idx = jnp.min(jnp.where(logits == m, lane, 128), axis=-1, keepdims=True)
        vals.append(m)
        idxs.append(idx)
        logits = jnp.where(lane == idx, -jnp.inf, logits)
    ex = [jnp.exp(v - vals[0]) for v in vals]
    inv = 1.0 / (ex[0] + ex[1] + ex[2] + ex[3])
    ti = jnp.zeros(lane.shape, I32)
    tw = jnp.zeros(lane.shape, F32)
    for kk in range(TOP_K):
        ti = jnp.where(lane == kk, idxs[kk], ti)
        tw = jnp.where(lane == kk, ex[kk] * inv, tw)
    ti_ref[0] = ti
    tw_ref[0] = tw


def _outproj(ya_lat, ya_ctx, yb, yc, yd, xcat, g1, scp, sh, w_bf16, lg, lb, wr, br, *, NL, NC, alpha):
    B, TT, _ = xcat.shape
    TR = NC
    n_lat_tiles = NL // TR
    mod_idx = lambda b, i: (jnp.where(i == n_lat_tiles, B, b), 0, 0)
    const2 = lambda b, i: (0, 0)
    row_blk = lambda c: pl.BlockSpec((1, TR, c), lambda b, i: (b, i, 0))
    mod = lambda: pl.BlockSpec((1, 1, D), mod_idx)
    return pl.pallas_call(
        functools.partial(_outproj_kernel, alpha=alpha, n_lat_tiles=n_lat_tiles),
        grid=(B, TT // TR),
        in_specs=[pl.BlockSpec((1, TR, GW), lambda b, i: (b, jnp.minimum(i, n_lat_tiles - 1), 0)),
                  pl.BlockSpec((1, TR, GW), lambda b, i: (b, 0, 0)),
                  row_blk(GW), row_blk(GW), row_blk(GW), row_blk(D), mod(), mod(), mod(),
                  pl.BlockSpec((D, D), const2), pl.BlockSpec((1, D), const2), pl.BlockSpec((1, D), const2),
                  pl.BlockSpec((D, 128), const2), pl.BlockSpec((1, 128), const2)],
        out_specs=[row_blk(D), pl.BlockSpec((1, TR, 8, 128), lambda b, i: (b, i, 0, 0)), row_blk(128), row_blk(128)],
        out_shape=[jax.ShapeDtypeStruct((B, TT, D), F32), jax.ShapeDtypeStruct((B, TT, 8, 128), F32),
                   jax.ShapeDtypeStruct((B, TT, 128), I32), jax.ShapeDtypeStruct((B, TT, 128), F32)],
        compiler_params=_cparams(("arbitrary", "arbitrary")),
        name="outproj_ln_router",
    )(ya_lat, ya_ctx, yb, yc, yd, xcat, g1, scp, sh, w_bf16, lg, lb, wr, br)


def _route(tidx, tw, tm):
    B, S, _ = tidx.shape
    e = tidx.reshape(B, S * TOP_K)
    order = jnp.argsort(e, axis=1, stable=True).astype(I32)
    tws = jnp.take_along_axis(tw.reshape(B, S * TOP_K), order, axis=1)
    counts = jnp.sum((e[:, :, None] == jnp.arange(N_EXP, dtype=I32)[None, None, :]).astype(I32), axis=1)
    starts = jnp.cumsum(counts, axis=1) - counts
    pad = ((0, 0), (0, tm))
    return (jnp.pad(order, pad)[:, None, :], jnp.pad(tws, pad)[:, None, :],
            counts.reshape(-1).astype(I32), starts.reshape(-1).astype(I32))


def _expert_kernel(cnt_ref, st_ref, ord_ref, tws_ref, h_ref, wg_ref, bg_ref, wu_ref, bu_ref, wd_ref, bd_ref,
                   acc_ref, xbuf, ybuf, *, tm):
    s = pl.program_id(0)
    e = pl.program_id(1)
    n = cnt_ref[s * N_EXP + e]
    st = st_ref[s * N_EXP + e]
    n_rows = h_ref.shape[0]

    @pl.when(e == 0)
    def _():
        acc_ref[...] = jnp.zeros(acc_ref.shape, F32)

    def tile(j, carry):
        base = st + j * tm
        m = n - j * tm

        def gather(r, c):
            xbuf[r] = h_ref[ord_ref[0, 0, base + r] >> 2]
            return c
        lax.fori_loop(0, tm, gather, 0, unroll=8)

        x = jnp.concatenate([xbuf[:, q, :] for q in range(8)], axis=1).astype(BF16)
        gate = jnp.minimum(_dot(x, wg_ref[0, 0]) + bg_ref[0, 0], SWIGLU_LIMIT)
        up = jnp.clip(_dot(x, wu_ref[0, 0]) + bu_ref[0, 0], -SWIGLU_LIMIT, SWIGLU_LIMIT)
        act = (up + 1.0) * gate * _sigmoid(SWIGLU_ALPHA * gate)
        y = _dot(act.astype(BF16), wd_ref[0, 0]) + bd_ref[0, 0]
        for q in range(8):
            ybuf[:, q, :] = y[:, q * 128:(q + 1) * 128]

        def scatter(grp, c):
            toks, ws = [], []
            for u in range(8):
                r = grp * 8 + u
                toks.append(jnp.where(r < m, ord_ref[0, 0, base + r] >> 2, n_rows))
                ws.append(jnp.where(r < m, tws_ref[0, 0, base + r], 0.0))
            old = [acc_ref[toks[u]] for u in range(8)]
            for u in range(8):
                acc_ref[toks[u]] = old[u] + ws[u] * ybuf[grp * 8 + u]
            return c
        lax.fori_loop(0, tm // 8, scatter, 0)
        return carry

    lax.fori_loop(0, (n + tm - 1) // tm, tile, 0)


def _experts(h4d, order, tws, counts, starts, wg, bg, wu, bu, wd, bd, *, tm, layer):
    B, S = h4d.shape[:2]
    n_ord = order.shape[-1]
    wspec = lambda: pl.BlockSpec((1, 1, D, D), lambda s, e, c, t: (layer, e, 0, 0))
    bspec = lambda: pl.BlockSpec((1, 1, 1, D), lambda s, e, c, t: (layer, e, 0, 0))
    smem = lambda: pl.BlockSpec((1, 1, n_ord), lambda s, e, c, t: (s, 0, 0), memory_space=pltpu.SMEM)
    tok = lambda rows, **kw: pl.BlockSpec((None, rows, 8, 128), lambda s, e, c, t: (s, 0, 0, 0), **kw)
    return pl.pallas_call(
        functools.partial(_expert_kernel, tm=tm),
        grid_spec=pltpu.PrefetchScalarGridSpec(
            num_scalar_prefetch=2,
            grid=(B, N_EXP),
            in_specs=[smem(), smem(), tok(S),
                      wspec(), bspec(), wspec(), bspec(), wspec(), bspec()],
            out_specs=tok(S + 8),
            scratch_shapes=[pltpu.VMEM((tm, 8, 128), F32), pltpu.VMEM((tm, 8, 128), F32)]),
        out_shape=jax.ShapeDtypeStruct((B, S + 8, 8, 128), F32),
        compiler_params=_cparams(("arbitrary", "arbitrary")),
        name="moe_experts",
    )(counts, starts, order, tws, h4d, wg, bg, wu, bu, wd, bd)


def _combine_kernel(f_ref, x_ref, g2_ref, lg_ref, lb_ref, o_ref, *, alpha):
    f = jnp.concatenate([f_ref[0, :, q, :] for q in range(8)], axis=1)
    o_ref[0] = _layer_norm(alpha * x_ref[0] + g2_ref[0] * f, lg_ref[...], lb_ref[...])


def _combine(f4d, x1, g2, lg, lb, *, NL, NC, alpha):
    B, TT, _ = x1.shape
    TR = NC
    n_lat_tiles = NL // TR
    mod_idx = lambda b, i: (jnp.where(i == n_lat_tiles, B, b), 0, 0)
    return pl.pallas_call(
        functools.partial(_combine_kernel, alpha=alpha),
        grid=(B, TT // TR),
        in_specs=[pl.BlockSpec((1, TR, 8, 128), lambda b, i: (b, i, 0, 0)),
                  pl.BlockSpec((1, TR, D), lambda b, i: (b, i, 0)), pl.BlockSpec((1, 1, D), mod_idx),
                  pl.BlockSpec((1, D), lambda b, i: (0, 0)), pl.BlockSpec((1, D), lambda b, i: (0, 0))],
        out_specs=pl.BlockSpec((1, TR, D), lambda b, i: (b, i, 0)),
        out_shape=jax.ShapeDtypeStruct((B, TT, D), F32),
        compiler_params=_cparams(("arbitrary", "arbitrary")),
        name="moe_combine_ln",
    )(f4d, x1, g2, lg, lb)


def _gdn_scan(dq, dk, dv, dz, dg, nw, ones_bd, *, NL, NC):
    return _scan_call(_gdn_kernel, "gdn_scan", (dq, dk, dv, dz, dg), nw, ones_bd, 256, NL=NL, NC=NC)


def _expert_tile_rows(tokens_per_sample):
    mean = tokens_per_sample * TOP_K / N_EXP
    return max(32, int(-(-(mean * 10 / 9) // 32) * 32))


def kernel(x, c, ctx, c_ctx, w_ada, b_ada, w_in, hy_conv_w, hy_conv_b, hy_w1, hy_b1, hy_w2, hy_b2, hy_w3, hy_b3, hy_w4, hy_freq, hy_d, sc_conv_w, gla_w_a, gla_b_a, gla_norm_w, gdn_conv_w, gdn_a_log, gdn_dt_bias, gdn_norm_w, w_out, ln1_g, ln1_b, w_router, b_router, w_gate, b_gate, w_up, b_up, w_down, b_down, ln2_g, ln2_b):
    B, NL, _ = x.shape
    NC = ctx.shape[1]
    TT = NL + NC
    T = B * TT
    depth = w_ada.shape[0]
    alpha = (2 * depth) ** 0.25
    assert NL % NC == 0 and NC % CHUNK == 0 and (NC & (NC - 1)) == 0

    xcat = jnp.concatenate([x, ctx], axis=1)
    n_mod = -(-(B + 1) // 8) * 8
    cc = jnp.concatenate([c, c_ctx[None], jnp.zeros((n_mod - B - 1, D), F32)], axis=0)
    mods = _ada_mods(cc, w_ada, b_ada)

    pp = _prep_inproj_params(w_in, hy_conv_w, hy_conv_b, hy_d, sc_conv_w, gla_w_a, gla_b_a, gdn_conv_w,
                             gdn_a_log, gdn_dt_bias)
    ones_bd = _block_ones(256, 64)
    hy_args = (jnp.pad(hy_w1, ((0, 0), (0, HY_HID - HY_EMB), (0, 0))), hy_b1[:, None], hy_w2, hy_b2[:, None],
               hy_w3, hy_b3[:, None], hy_w4, hy_freq)
    tabs_l = _dft_tables(NL)
    tabs_c = _dft_tables(NC)
    spec_l = _hyena_spectrum(NL, tabs_l[0], tabs_l[1], *hy_args)
    spec_c = _hyena_spectrum(NC, tabs_c[0], tabs_c[1], *hy_args)
    tabs_l = tuple(t.astype(BF16) for t in tabs_l)
    tabs_c = tuple(t.astype(BF16) for t in tabs_c)

    w_out_b = w_out.astype(BF16)
    wr = _pad_cols(w_router, 128)
    br = jnp.pad(b_router, ((0, 0), (0, 128 - N_EXP)), constant_values=NEG_BIG)[:, None, :]
    wg, wu, wd = w_gate.astype(BF16), w_up.astype(BF16), w_down.astype(BF16)
    tm = _expert_tile_rows(TT)

    for l in range(depth):
        sh1, sc1, g1, sh2, sc2, g2 = (mods[l, :, k * D:(k + 1) * D][:, None, :] for k in range(6))
        zb, e, x0, ysc, gqk, gv, gr, gg, dq, dk, dv, dz, dg = _inproj(
            xcat, 1.0 + sc1, sh1, pp["w"][l], pp["hyw"][l], pp["hyb"][l], pp["hyd"][l], pp["scw"][l],
            pp["glaw"][l], pp["glab"][l], pp["gdnw"][l], pp["gdna"][l], pp["gdnd"][l], ones_bd, NL=NL, NC=NC)
        ya_lat = _hyconv(zb, e, x0, spec_l[l:l + 1], tabs_l, L=NL, row_blk=0)
        ya_ctx = _hyconv(zb, e, x0, spec_c[l:l + 1], tabs_c, L=NC, row_blk=NL // NC)
        yc = _gla_scan(gqk, gv, gr, gg, jnp.tile(gla_norm_w[l], 4)[None], ones_bd, NL=NL, NC=NC)
        yd = _gdn_scan(dq, dk, dv, dz, dg, jnp.tile(gdn_norm_w[l], 4)[None], ones_bd, NL=NL, NC=NC)
        x1, h, ti, tw = _outproj(ya_lat, ya_ctx, ysc, yc, yd, xcat, g1, 1.0 + sc2, sh2, w_out_b[l], ln1_g[l][None],
                                 ln1_b[l][None], wr[l], br[l], NL=NL, NC=NC, alpha=alpha)
        n_routed = NL if l == depth - 1 else TT
        order, tws, counts, starts = _route(ti[:, :n_routed, :TOP_K], tw[:, :n_routed, :TOP_K], tm)
        f = _experts(h, order, tws, counts, starts, wg, b_gate[:, :, None, :], wu, b_up[:, :, None, :], wd,
                     b_down[:, :, None, :], tm=tm, layer=l)
        xcat = _combine(f, x1, g2, ln2_g[l][None], ln2_b[l][None], NL=NL, NC=NC, alpha=alpha)
    return xcat[:, :NL]
```
